```python
import jax, jax.numpy as jnp
from jax import lax
import numpy as np

D_MODEL = 2048
BATCH = 8
SEQ = 4096
DEPTH = 2

CHUNK = 64

HG_HEADS = 8
HG_DK = 128
HG_DV = 128
HG_KW = HG_HEADS * HG_DK
HG_WIDTH = HG_HEADS * HG_DV
RW_HEADS = 16
RW_HEAD = 64
RW_WIDTH = RW_HEADS * RW_HEAD
RW_DECAY_RANK = 64
RW_A_RANK = 64
RW_GATE_RANK = 160
RW_GN_EPS = 64e-5
RW_COLS = 3 * RW_WIDTH + RW_DECAY_RANK + RW_A_RANK + RW_GATE_RANK
EVEN_COLS = 2 * HG_KW + 2 * HG_WIDTH + RW_COLS
EVEN_MIX = HG_WIDTH + RW_WIDTH

SSD_INNER = 2 * D_MODEL
SSD_HEADDIM = 64
SSD_HEADS = SSD_INNER // SSD_HEADDIM
SSD_GROUPS = 8
SSD_HPG = SSD_HEADS // SSD_GROUPS
SSD_STATE = 128
SSD_CONV = 4
SSD_CONV_DIM = SSD_INNER + 2 * SSD_GROUPS * SSD_STATE
ODD_COLS = SSD_INNER + SSD_CONV_DIM + SSD_HEADS

N_EXPERTS = 64
TOP_K = 8
N_GROUPS = 8
TOPK_GROUPS = 4
EXPERT_DIM = 512
SHARED_DIM = 512
ROUTED_SCALE = 2.5

NORM_EPS = 1e-5
DN_ALPHA = (2 * DEPTH) ** 0.25
DN_BETA = (8 * DEPTH) ** -0.25
N_EVEN = (DEPTH + 1) // 2
N_ODD = DEPTH // 2

kernel_name = "hgrn2_rwkv7_mamba2_moe_deepnorm_trunk"


def chunk_major(t):
    b, s = t.shape[:2]
    return jnp.moveaxis(t.reshape(b, s // CHUNK, CHUNK, *t.shape[2:]), 1, 0)


def chunk_major_back(t):
    t = jnp.moveaxis(t, 0, 1)
    return t.reshape(t.shape[0], t.shape[1] * t.shape[2], *t.shape[3:])


def token_shift(t):
    return jnp.pad(t, ((0, 0), (1, 0), (0, 0)))[:, :-1]


def group_rms(x, groups):
    shp = x.shape
    xf = x.astype(jnp.float32).reshape(*shp[:-1], groups, shp[-1] // groups)
    xf = xf * lax.rsqrt(jnp.mean(xf * xf, axis=-1, keepdims=True) + NORM_EPS)
    return xf.reshape(shp).astype(x.dtype)


def group_layer_norm(x, groups, eps):
    shp = x.shape
    xf = x.astype(jnp.float32).reshape(*shp[:-1], groups, shp[-1] // groups)
    mu = jnp.mean(xf, axis=-1, keepdims=True)
    var = jnp.mean(jnp.square(xf - mu), axis=-1, keepdims=True)
    return ((xf - mu) * lax.rsqrt(var + eps)).reshape(shp).astype(x.dtype)


def layer_norm(x, g, b):
    return group_layer_norm(x, 1, NORM_EPS) * g + b


def causal_dwconv(x, w, b):
    y = lax.conv_general_dilated(
        x, w[:, None, :], window_strides=(1,), padding=[(w.shape[0] - 1, 0)],
        dimension_numbers=('NWC', 'WIO', 'NWC'), feature_group_count=x.shape[-1])
    return y + b


def hgrn2_mixer(q, f_pre, i, g, lb, gn_w):
    b, s, _ = q.shape
    f32 = jnp.float32
    f_pre = f_pre.astype(f32)
    log_f = jnp.log(lb + (1.0 - lb) * jax.nn.sigmoid(f_pre))
    k = (1.0 - lb) * jax.nn.sigmoid(-f_pre)
    split = lambda t: chunk_major(t.astype(f32).reshape(b, s, HG_HEADS, -1))
    qc, kc, fc, vc = split(q), split(k), split(log_f), split(i)
    causal = jnp.tril(jnp.ones((CHUNK, CHUNK), dtype=bool))

    def step(state, inp):
        q_c, k_c, lf_c, v_c = inp
        bc = jnp.cumsum(lf_c, axis=1)
        o_inter = jnp.einsum('bthk,bhkv->bthv', q_c * jnp.exp(bc), state)
        diff = bc[:, :, None] - bc[:, None, :]
        dec = jnp.exp(jnp.where(causal[None, :, :, None, None], diff, -jnp.inf))
        att = jnp.einsum('bthk,bshk,btshk->btsh', q_c, k_c, dec)
        o_intra = jnp.einsum('btsh,bshv->bthv', att, v_c)
        blast = bc[:, -1]
        state = jnp.exp(blast)[..., None] * state + jnp.einsum(
            'bshk,bshv->bhkv', k_c * jnp.exp(blast[:, None] - bc), v_c)
        return state, o_inter + o_intra

    state0 = jnp.zeros((b, HG_HEADS, HG_DK, HG_DV), f32)
    _, o = lax.scan(step, state0, (qc, kc, fc, vc))
    o = chunk_major_back(o).reshape(b, s, HG_WIDTH)
    return group_rms(o, HG_HEADS).astype(q.dtype) * gn_w * jax.nn.silu(g)


def rwkv7_mixer(p, mix, w0, w2, a0, a2, g2, k_k, k_a, r_k, ln_w, ln_b):
    b, s, _ = p.shape
    f32 = jnp.float32
    p = p + (token_shift(p) - p) * mix
    cuts = [RW_WIDTH, 2 * RW_WIDTH, 3 * RW_WIDTH, 3 * RW_WIDTH + RW_DECAY_RANK,
            3 * RW_WIDTH + RW_DECAY_RANK + RW_A_RANK]
    r, k, v, w_low, a_low, g_low = jnp.split(p, cuts, axis=-1)
    w_log = -jax.nn.softplus(-(w0 + jnp.tanh(w_low) @ w2).astype(f32)) - 0.5
    decay = jnp.exp(-jnp.exp(w_log))
    a = jax.nn.sigmoid((a0 + a_low @ a2).astype(f32))
    g = jax.nn.sigmoid(g_low) @ g2
    heads = lambda t: t.astype(f32).reshape(b, s, RW_HEADS, RW_HEAD)
    kk = heads(k * k_k)
    kk = kk / jnp.maximum(jnp.linalg.norm(kk, axis=-1, keepdims=True), 1e-12)
    kh = heads(k.astype(f32) * (1.0 + (a - 1.0) * k_a))
    rh, vh, wh, ah = heads(r), heads(v), heads(decay), heads(a)
    tm = lambda t: jnp.moveaxis(t, 1, 0)

    def step(state, inp):
        r_t, w_t, k_t, v_t, a_t, b_t = inp
        sa = jnp.einsum('bhvk,bhk->bhv', state, a_t)
        state = (state * w_t[:, :, None, :] + sa[..., None] * b_t[:, :, None, :]
                 + v_t[..., None] * k_t[:, :, None, :])
        return state, jnp.einsum('bhvk,bhk->bhv', state, r_t)

    state0 = jnp.zeros((b, RW_HEADS, RW_HEAD, RW_HEAD), f32)
    _, y = lax.scan(step, state0, (tm(rh), tm(wh), tm(kh), tm(vh), tm(-kk), tm(kk * ah)))
    y = jnp.moveaxis(y, 0, 1).reshape(b, s, RW_WIDTH)
    y = group_layer_norm(y, RW_HEADS, RW_GN_EPS) * ln_w + ln_b
    bonus = jnp.sum(rh * kh * r_k, axis=-1, keepdims=True) * vh
    y = y + bonus.reshape(b, s, RW_WIDTH)
    return (y * g).astype(p.dtype)


def even_mixer(h, lb, in_proj, out_proj, hg_gn_w, rw_mix, rw_w0, rw_w2, rw_a0,
               rw_a2, rw_g2, rw_kk, rw_ka, rw_rk, rw_ln_w, rw_ln_b):
    p = h @ in_proj
    cuts = [HG_KW, 2 * HG_KW, 2 * HG_KW + HG_WIDTH, 2 * HG_KW + 2 * HG_WIDTH]
    hq, hf, hi, hg, prw = jnp.split(p, cuts, axis=-1)
    o_a = hgrn2_mixer(hq, hf, hi, hg, lb, hg_gn_w)
    o_b = rwkv7_mixer(prw, rw_mix, rw_w0, rw_w2, rw_a0, rw_a2, rw_g2, rw_kk, rw_ka,
                      rw_rk, rw_ln_w, rw_ln_b)
    return jnp.concatenate([o_a, o_b], axis=-1) @ out_proj


def mamba2_mixer(h, in_proj, conv_w, conv_b, dt_bias, a_log, d_skip, norm_w, out_proj):
    b, s, _ = h.shape
    f32 = jnp.float32
    p = h @ in_proj
    z, xbc, dt = jnp.split(p, [SSD_INNER, SSD_INNER + SSD_CONV_DIM], axis=-1)
    xbc = jax.nn.silu(causal_dwconv(xbc, conv_w, conv_b))
    xs, bm, cm = jnp.split(xbc, [SSD_INNER, SSD_INNER + SSD_GROUPS * SSD_STATE], axis=-1)
    dt = jax.nn.softplus((dt + dt_bias).astype(f32))
    a_head = -jnp.exp(a_log.astype(f32)).reshape(SSD_GROUPS, SSD_HPG)
    xc = chunk_major(xs.astype(f32).reshape(b, s, SSD_GROUPS, SSD_HPG, SSD_HEADDIM))
    dtc = chunk_major(dt.reshape(b, s, SSD_GROUPS, SSD_HPG))
    bc = chunk_major(bm.astype(f32).reshape(b, s, SSD_GROUPS, SSD_STATE))
    cc = chunk_major(cm.astype(f32).reshape(b, s, SSD_GROUPS, SSD_STATE))
    causal = jnp.tril(jnp.ones((CHUNK, CHUNK), dtype=bool))

    def step(state, inp):
        x_c, dt_c, b_c, c_c = inp
        acum = jnp.cumsum(dt_c * a_head, axis=1)
        seg = acum[:, :, None] - acum[:, None, :]
        lmat = jnp.exp(jnp.where(causal[None, :, :, None, None], seg, -jnp.inf))
        xdt = x_c * dt_c[..., None]
        cb = jnp.einsum('btgn,bsgn->btsg', c_c, b_c)
        y = jnp.einsum('btsg,btsgj,bsgjp->btgjp', cb, lmat, xdt)
        y = y + jnp.einsum('btgn,bgjpn->btgjp', c_c, state) * jnp.exp(acum)[..., None]
        alast = acum[:, -1]
        state = jnp.exp(alast)[..., None, None] * state + jnp.einsum(
            'bsgn,bsgjp->bgjpn', b_c, xdt * jnp.exp(alast[:, None] - acum)[..., None])
        return state, y

    state0 = jnp.zeros((b, SSD_GROUPS, SSD_HPG, SSD_HEADDIM, SSD_STATE), f32)
    _, y = lax.scan(step, state0, (xc, dtc, bc, cc))
    y = chunk_major_back(y).reshape(b, s, SSD_HEADS, SSD_HEADDIM)
    y = y + d_skip[:, None] * xs.astype(f32).reshape(b, s, SSD_HEADS, SSD_HEADDIM)
    y = y.reshape(b, s, SSD_INNER).astype(h.dtype)
    y = group_rms(y * jax.nn.silu(z), SSD_GROUPS) * norm_w
    return y @ out_proj


def moe_ffn(x2, router, e_bias, w_gate, w_up, w_down, s_gate, s_up, s_down):
    t = x2.shape[0]
    scores = jax.nn.sigmoid((x2 @ router).astype(jnp.float32))
    biased = scores + e_bias.astype(jnp.float32)
    group_score = lax.top_k(biased.reshape(t, N_GROUPS, -1), 2)[0].sum(-1)
    _, gidx = lax.top_k(group_score, TOPK_GROUPS)
    gmask = jax.nn.one_hot(gidx, N_GROUPS, dtype=jnp.float32).sum(1) > 0
    emask = jnp.repeat(gmask, N_EXPERTS // N_GROUPS, axis=1)
    _, eidx = lax.top_k(jnp.where(emask, biased, -jnp.inf), TOP_K)
    wts = jnp.take_along_axis(scores, eidx, axis=1)
    wts = wts / jnp.sum(wts, axis=-1, keepdims=True) * ROUTED_SCALE
    gates = jnp.einsum('tk,tke->et', wts,
                       jax.nn.one_hot(eidx, N_EXPERTS, dtype=jnp.float32)).astype(x2.dtype)

    def expert(acc, inp):
        wg, wu, wd, ge = inp
        hdn = jax.nn.silu(x2 @ wg) * (x2 @ wu) * ge[:, None]
        return acc + hdn @ wd, None

    routed, _ = lax.scan(expert, jnp.zeros_like(x2), (w_gate, w_up, w_down, gates))
    shared = (jax.nn.silu(x2 @ s_gate) * (x2 @ s_up)) @ s_down
    return routed + shared


def setup_inputs(seed: int = 0) -> dict:
    key = jax.random.key(seed)
    ks = iter(jax.random.split(key, 48))
    f32 = jnp.float32
    nrm = lambda shape, scale: jax.random.normal(next(ks), shape, f32) * scale
    unif = lambda shape, lo, hi: jax.random.uniform(next(ks), shape, f32, lo, hi)
    dt0 = jnp.exp(unif((N_ODD, SSD_HEADS), float(np.log(1e-3)), float(np.log(1e-1))))
    return {
        "x": nrm((BATCH, SEQ, D_MODEL), 1.0),
        "hg_lb": nrm((DEPTH + 1, HG_KW), 0.1),
        "ev_in_proj": nrm((N_EVEN, D_MODEL, EVEN_COLS), D_MODEL ** -0.5),
        "ev_out_proj": nrm((N_EVEN, EVEN_MIX, D_MODEL), EVEN_MIX ** -0.5 * DN_BETA),
        "hg_gn_w": 1.0 + nrm((N_EVEN, HG_WIDTH), 0.02),
        "rw_mix": unif((N_EVEN, RW_COLS), 0.0, 1.0),
        "rw_w0": unif((N_EVEN, RW_WIDTH), -6.0, -1.0),
        "rw_w2": nrm((N_EVEN, RW_DECAY_RANK, RW_WIDTH), 0.1 * RW_DECAY_RANK ** -0.5),
        "rw_a0": nrm((N_EVEN, RW_WIDTH), 0.1),
        "rw_a2": nrm((N_EVEN, RW_A_RANK, RW_WIDTH), 0.1 * RW_A_RANK ** -0.5),
        "rw_g2": nrm((N_EVEN, RW_GATE_RANK, RW_WIDTH), RW_GATE_RANK ** -0.5),
        "rw_kk": 0.85 + nrm((N_EVEN, RW_WIDTH), 0.02),
        "rw_ka": 1.0 + nrm((N_EVEN, RW_WIDTH), 0.02),
        "rw_rk": nrm((N_EVEN, RW_HEADS, RW_HEAD), 0.1),
        "rw_ln_w": 1.0 + nrm((N_EVEN, RW_WIDTH), 0.02),
        "rw_ln_b": nrm((N_EVEN, RW_WIDTH), 0.02),
        "od_in_proj": nrm((N_ODD, D_MODEL, ODD_COLS), D_MODEL ** -0.5),
        "od_conv_w": nrm((N_ODD, SSD_CONV, SSD_CONV_DIM), SSD_CONV ** -0.5),
        "od_conv_b": nrm((N_ODD, SSD_CONV_DIM), 0.02),
        "od_dt_bias": dt0 + jnp.log(-jnp.expm1(-dt0)),
        "od_a_log": jnp.log(unif((N_ODD, SSD_HEADS), 1.0, 16.0)),
        "od_d": 1.0 + nrm((N_ODD, SSD_HEADS), 0.02),
        "od_norm_w": 1.0 + nrm((N_ODD, SSD_INNER), 0.02),
        "od_out_proj": nrm((N_ODD, SSD_INNER, D_MODEL), SSD_INNER ** -0.5 * DN_BETA),
        "moe_router": nrm((DEPTH, D_MODEL, N_EXPERTS), D_MODEL ** -0.5),
        "moe_bias": nrm((DEPTH, N_EXPERTS), 0.01),
        "moe_w_gate": nrm((DEPTH, N_EXPERTS, D_MODEL, EXPERT_DIM), D_MODEL ** -0.5),
        "moe_w_up": nrm((DEPTH, N_EXPERTS, D_MODEL, EXPERT_DIM), D_MODEL ** -0.5),
        "moe_w_down": nrm((DEPTH, N_EXPERTS, EXPERT_DIM, D_MODEL), EXPERT_DIM ** -0.5 * DN_BETA),
        "sh_w_gate": nrm((DEPTH, D_MODEL, SHARED_DIM), D_MODEL ** -0.5),
        "sh_w_up": nrm((DEPTH, D_MODEL, SHARED_DIM), D_MODEL ** -0.5),
        "sh_w_down": nrm((DEPTH, SHARED_DIM, D_MODEL), SHARED_DIM ** -0.5 * DN_BETA),
        "ln1_g": 1.0 + nrm((DEPTH, D_MODEL), 0.02),
        "ln1_b": nrm((DEPTH, D_MODEL), 0.02),
        "ln2_g": 1.0 + nrm((DEPTH, D_MODEL), 0.02),
        "ln2_b": nrm((DEPTH, D_MODEL), 0.02),
    }


def reference(x, hg_lb, ev_in_proj, ev_out_proj, hg_gn_w, rw_mix, rw_w0, rw_w2, rw_a0,
              rw_a2, rw_g2, rw_kk, rw_ka, rw_rk, rw_ln_w, rw_ln_b, od_in_proj, od_conv_w,
              od_conv_b, od_dt_bias, od_a_log, od_d, od_norm_w, od_out_proj, moe_router,
              moe_bias, moe_w_gate, moe_w_up, moe_w_down, sh_w_gate, sh_w_up, sh_w_down,
              ln1_g, ln1_b, ln2_g, ln2_b):
    lbs = jnp.cumsum(jax.nn.softmax(hg_lb.astype(jnp.float32), axis=0), axis=0)
    h = x
    for l in range(DEPTH):
        if l % 2 == 0:
            e = l // 2
            mix = even_mixer(h, lbs[l], ev_in_proj[e], ev_out_proj[e], hg_gn_w[e], rw_mix[e],
                             rw_w0[e], rw_w2[e], rw_a0[e], rw_a2[e], rw_g2[e], rw_kk[e],
                             rw_ka[e], rw_rk[e], rw_ln_w[e], rw_ln_b[e])
        else:
            o = l // 2
            mix = mamba2_mixer(h, od_in_proj[o], od_conv_w[o], od_conv_b[o], od_dt_bias[o],
                               od_a_log[o], od_d[o], od_norm_w[o], od_out_proj[o])
        h = layer_norm(DN_ALPHA * h + mix, ln1_g[l], ln1_b[l])
        b, s, d = h.shape
        ff = moe_ffn(h.reshape(b * s, d), moe_router[l], moe_bias[l], moe_w_gate[l],
                     moe_w_up[l], moe_w_down[l], sh_w_gate[l], sh_w_up[l],
                     sh_w_down[l]).reshape(b, s, d)
        h = layer_norm(DN_ALPHA * h + ff, ln2_g[l], ln2_b[l])
    return h
```

```python
import functools
import math

import jax
import jax.numpy as jnp
import numpy as np
from jax import lax
from jax.experimental import pallas as pl
from jax.experimental.pallas import tpu as pltpu

F32 = jnp.float32
BF16 = jnp.bfloat16

D_MODEL = 2048
DEPTH = 2
CHUNK = 64

HG_HEADS = 8
HG_D = 128
HG_W = HG_HEADS * HG_D
RW_HEADS = 16
RW_N = 64
RW_W = RW_HEADS * RW_N
RW_DECAY_RANK = 64
RW_A_RANK = 64
RW_GATE_RANK = 160
RW_GN_EPS = 64e-5
RW_LOW_PAD = 512

SSD_INNER = 2 * D_MODEL
SSD_P = 64
SSD_HEADS = SSD_INNER // SSD_P
SSD_GROUPS = 8
SSD_HPG = SSD_HEADS // SSD_GROUPS
SSD_N = 128
SSD_CONV = 4
SSD_BC = SSD_GROUPS * SSD_N
SSD_CONV_DIM = SSD_INNER + 2 * SSD_BC
SSD_GW = SSD_HPG * SSD_P

N_EXPERTS = 64
TOP_K = 8
N_GROUPS = 8
TOPK_GROUPS = 4
EXPERT_DIM = 512
ROUTED_SCALE = 2.5

NORM_EPS = 1e-5
DN_ALPHA = (2 * DEPTH) ** 0.25

VMEM_LIMIT = 56 * 1024 * 1024


def _cparams(*sem):
    return pltpu.CompilerParams(dimension_semantics=sem, vmem_limit_bytes=VMEM_LIMIT)


def _const_spec(shape):
    nd = len(shape)
    return pl.BlockSpec(shape, lambda *_: (0,) * nd, pipeline_mode=pl.Buffered(1))


def _split3(x):
    hi = x.astype(BF16)
    r1 = x - hi.astype(F32)
    mid = r1.astype(BF16)
    lo = (r1 - mid.astype(F32)).astype(BF16)
    return hi, mid, lo


def _dot(a, b):
    return jnp.dot(a, b, preferred_element_type=F32)


def _dot_exact_lhs(m_bf16, x):
    hi, mid, lo = _split3(x)
    return _dot(m_bf16, hi) + _dot(m_bf16, mid) + _dot(m_bf16, lo)


def _dot_exact_rhs(x, m_bf16):
    hi, mid, lo = _split3(x)
    return _dot(hi, m_bf16) + _dot(mid, m_bf16) + _dot(lo, m_bf16)


def _sigmoid(x):
    return 1.0 / (1.0 + jnp.exp(-x))


def _silu(x):
    return x * _sigmoid(x)


def _softplus(x):
    return jnp.maximum(x, 0.0) + jnp.log(1.0 + jnp.exp(-jnp.abs(x)))


def _mm_kernel(x_ref, w_ref, o_ref):
    o_ref[...] = _dot(x_ref[...], w_ref[...]).astype(o_ref.dtype)


def matmul(x, w, out_dtype, tm=1024, tn=512):
    m, k = x.shape
    n = w.shape[1]
    tm = min(tm, m)
    tn = min(tn, n)
    assert m % tm == 0 and n % tn == 0
    return pl.pallas_call(
        _mm_kernel,
        grid=(m // tm, n // tn),
        in_specs=[pl.BlockSpec((tm, k), lambda i, j: (i, 0)),
                  pl.BlockSpec((k, tn), lambda i, j: (0, j))],
        out_specs=pl.BlockSpec((tm, tn), lambda i, j: (i, j)),
        out_shape=jax.ShapeDtypeStruct((m, n), out_dtype),
        compiler_params=_cparams("parallel", "arbitrary"),
        name="matmul",
    )(x, w)


def _layer_norm_rows(y, g, b):
    mu = jnp.mean(y, axis=-1, keepdims=True)
    yc = y - mu
    var = jnp.mean(yc * yc, axis=-1, keepdims=True)
    return yc * lax.rsqrt(var + NORM_EPS) * g + b


def _bf16_bits(x):
    u = lax.bitcast_convert_type(x, jnp.uint32)
    u = u + jnp.uint32(0x7FFF) + ((u >> 16) & jnp.uint32(1))
    return u & jnp.uint32(0xFFFF0000)


def _pack_halves(lo, hi):
    return (_bf16_bits(lo) >> 16) | _bf16_bits(hi)


def _unpack_halves(u):
    lo = lax.bitcast_convert_type(u << 16, F32)
    hi = lax.bitcast_convert_type(u & jnp.uint32(0xFFFF0000), F32)
    return lo, hi


HALF = D_MODEL // 2


def _mm_ln_kernel(x_ref, w_ref, h_ref, g_ref, b_ref, o32_ref, o16_ref, opk_ref):
    y = _dot(x_ref[...], w_ref[...]) + DN_ALPHA * h_ref[...]
    o = _layer_norm_rows(y, g_ref[...], b_ref[...])
    o32_ref[...] = o
    o16_ref[...] = o.astype(BF16)
    opk_ref[...] = _pack_halves(o[:, :HALF], o[:, HALF:])


def matmul_residual_ln(x, w, h, g, b, tm=256):
    m, k = x.shape
    d = w.shape[1]
    tm = min(tm, m)
    assert m % tm == 0
    return pl.pallas_call(
        _mm_ln_kernel,
        grid=(m // tm,),
        in_specs=[pl.BlockSpec((tm, k), lambda i: (i, 0)),
                  _const_spec((k, d)),
                  pl.BlockSpec((tm, d), lambda i: (i, 0)),
                  _const_spec((1, d)),
                  _const_spec((1, d))],
        out_specs=[pl.BlockSpec((tm, d), lambda i: (i, 0)),
                   pl.BlockSpec((tm, d), lambda i: (i, 0)),
                   pl.BlockSpec((tm, d // 2), lambda i: (i, 0))],
        out_shape=[jax.ShapeDtypeStruct((m, d), F32), jax.ShapeDtypeStruct((m, d), BF16),
                   jax.ShapeDtypeStruct((m, d // 2), jnp.uint32)],
        compiler_params=_cparams("parallel"),
        name="matmul_residual_ln",
    )(x, w, h, g.reshape(1, d), b.reshape(1, d))


HG_LEVELS = (32, 16, 8, 4, 2, 1)


def _hgrn2_constants():
    t = np.arange(CHUNK)[:, None]
    u = np.arange(CHUNK)[None, :]
    blocks = [(u <= t)]
    for b in HG_LEVELS:
        start = (t // b) * b
        end = start + b - 1
        blocks.append((u >= start) & (u <= t))
        blocks.append((u > t) & (u <= end))
    blocks.append(u > t)
    cm = np.concatenate(blocks, axis=0).astype(np.float32)
    s = u
    masks = [((t // (2 * b)) == (s // (2 * b))) & ((t // b) % 2 == 1) & ((s // b) % 2 == 0) for b in HG_LEVELS]
    return cm, np.stack(masks).astype(np.float32)


def _hgrn2_kernel(q_ref, f_ref, i_ref, g_ref, lb_ref, gnw_ref, cm_ref, mask_ref, o_ref, st_ref, *, n_chunks):
    @pl.when(pl.program_id(2) == 0)
    def _():
        st_ref[...] = jnp.zeros_like(st_ref)

    lb = lb_ref[...]
    gnw = gnw_ref[...]
    cm = cm_ref[...]
    nt = (((1,), (1,)), ((), ()))

    def chunk(c, carry):
        rows = pl.ds(pl.multiple_of(c * CHUNK, CHUNK), CHUNK)
        q = q_ref[rows, :]
        fp = f_ref[rows, :]
        v = i_ref[rows, :]
        g = g_ref[rows, :]
        lf = jnp.log(lb + (1.0 - lb) * _sigmoid(fp))
        k = (1.0 - lb) * _sigmoid(-fp)
        seg = _dot_exact_lhs(cm, lf)
        bc = seg[0:CHUNK]
        st = st_ref[...]
        o = lax.dot_general((q * jnp.exp(bc)).astype(BF16), st.astype(BF16), nt, preferred_element_type=F32)
        att = jnp.zeros((CHUNK, CHUNK), F32)
        for li in range(len(HG_LEVELS)):
            base = CHUNK + 2 * CHUNK * li
            qs = q * jnp.exp(seg[base:base + CHUNK])
            ks = k * jnp.exp(seg[base + CHUNK:base + 2 * CHUNK])
            a_l = lax.dot_general(qs.astype(BF16), ks.astype(BF16), nt, preferred_element_type=F32)
            att = att + a_l * mask_ref[li]
        vb = v.astype(BF16)
        o = o + _dot(att.astype(BF16), vb) + jnp.sum(q * k, axis=-1, keepdims=True) * v
        tail = CHUNK + 2 * CHUNK * len(HG_LEVELS)
        kd = k * jnp.exp(seg[tail:tail + CHUNK])
        st_ref[...] = st * jnp.exp(bc[CHUNK - 1:CHUNK]) + _dot(v.T.astype(BF16), kd.astype(BF16))
        ms = jnp.mean(o * o, axis=-1, keepdims=True)
        o_ref[rows, :] = (o * lax.rsqrt(ms + NORM_EPS) * gnw * _silu(g)).astype(o_ref.dtype)
        return carry

    lax.fori_loop(0, n_chunks, chunk, 0)


def hgrn2(p_hg, lb, gn_w, batch, seq, tb=256):
    tb = min(tb, seq)
    nt = seq // tb
    cm, masks = _hgrn2_constants()
    spec = lambda off: pl.BlockSpec((tb, HG_D), lambda b, h, t: (b * nt + t, off + h))
    return pl.pallas_call(
        functools.partial(_hgrn2_kernel, n_chunks=tb // CHUNK),
        grid=(batch, HG_HEADS, nt),
        in_specs=[spec(0), spec(HG_HEADS), spec(2 * HG_HEADS), spec(3 * HG_HEADS),
                  pl.BlockSpec((1, HG_D), lambda b, h, t: (0, h)),
                  pl.BlockSpec((1, HG_D), lambda b, h, t: (0, h)),
                  _const_spec(cm.shape), _const_spec(masks.shape)],
        out_specs=pl.BlockSpec((tb, HG_D), lambda b, h, t: (b * nt + t, h)),
        out_shape=jax.ShapeDtypeStruct((batch * seq, HG_W), BF16),
        scratch_shapes=[pltpu.VMEM((HG_D, HG_D), F32)],
        compiler_params=_cparams("parallel", "parallel", "arbitrary"),
        name="hgrn2",
    )(p_hg, p_hg, p_hg, p_hg, lb.reshape(1, HG_W), gn_w.reshape(1, HG_W),
      jnp.asarray(cm, BF16), jnp.asarray(masks))


def _rwkv_prep_kernel(pm_ref, pmprev_ref, plo_ref, ploprev_ref, mixm_ref, mixl_ref, w0_ref, w2_ref, a0_ref,
                      a2_ref, g2_ref, r_o, k_o, v_o, w_o, a_o, g_o, *, tb, seq):
    first = (pl.program_id(0) * tb) % seq == 0
    keep = jnp.where(first, 0.0, 1.0)

    def shift_mix(cur, prev_blk, mix):
        row = lax.broadcasted_iota(jnp.int32, cur.shape, 0)
        shifted = jnp.where(row == 0, prev_blk[7:8, :] * keep, pltpu.roll(cur, 1, 0))
        return cur + (shifted - cur) * mix

    for idx, out in enumerate((r_o, k_o, v_o)):
        cols = slice(idx * RW_W, (idx + 1) * RW_W)
        out[...] = shift_mix(pm_ref[:, cols], pmprev_ref[:, cols], mixm_ref[:, cols])
    low = shift_mix(plo_ref[...], ploprev_ref[...], mixl_ref[...])
    wl = jnp.tanh(low[:, 0:128]).astype(BF16)
    al = low[:, 128:256].astype(BF16)
    gl = _sigmoid(low[:, 256:512]).astype(BF16)
    w_log = -_softplus(-(w0_ref[...] + _dot(wl, w2_ref[...]))) - 0.5
    w_o[...] = jnp.exp(-jnp.exp(w_log))
    a_o[...] = _sigmoid(a0_ref[...] + _dot(al, a2_ref[...]))
    g_o[...] = _dot(gl, g2_ref[...])


def rwkv_prep(p_main, p_low, mix_main, mix_low, w0, w2p, a0, a2p, g2p, seq, tb=256):
    m = p_main.shape[0]
    tb = min(tb, seq)
    prev = lambda i: (jnp.maximum(i * (tb // 8) - 1, 0), 0)
    out = jax.ShapeDtypeStruct((m, RW_W), F32)
    ospec = pl.BlockSpec((tb, RW_W), lambda i: (i, 0))
    return pl.pallas_call(
        functools.partial(_rwkv_prep_kernel, tb=tb, seq=seq),
        grid=(m // tb,),
        in_specs=[pl.BlockSpec((tb, 3 * RW_W), lambda i: (i, 0)),
                  pl.BlockSpec((8, 3 * RW_W), prev),
                  pl.BlockSpec((tb, RW_LOW_PAD), lambda i: (i, 0)),
                  pl.BlockSpec((8, RW_LOW_PAD), prev),
                  _const_spec((1, 3 * RW_W)), _const_spec((1, RW_LOW_PAD)),
                  _const_spec((1, RW_W)), _const_spec((128, RW_W)),
                  _const_spec((1, RW_W)), _const_spec((128, RW_W)),
                  _const_spec((256, RW_W))],
        out_specs=[ospec] * 6,
        out_shape=[out] * 6,
        compiler_params=_cparams("parallel"),
        name="rwkv_prep",
    )(p_main, p_main, p_low, p_low, mix_main, mix_low, w0, w2p, a0, a2p, g2p)


def _rwkv_scan_kernel(r_ref, k_ref, v_ref, w_ref, a_ref, g_ref, kkp_ref, kap_ref, rkp_ref, lnw_ref, lnb_ref,
                      o_ref, s_ref, vec_ref, *, tt):
    @pl.when(pl.program_id(0) == 0)
    def _():
        s_ref[...] = jnp.zeros_like(s_ref)

    kkp = kkp_ref[...]
    kap = kap_ref[...]
    rkp = rkp_ref[...]
    lnw = lnw_ref[...]
    lnb = lnb_ref[...]

    def step(t, carry):
        kt = k_ref[t]
        at = a_ref[t]
        rt = r_ref[t]
        vt = v_ref[t]
        kk = kt * kkp
        nrm = jnp.sqrt(jnp.sum(kk * kk, axis=0, keepdims=True))
        kk = kk / jnp.maximum(nrm, 1e-12)
        kh = kt * (1.0 + (at - 1.0) * kap)
        vec_ref[0] = -kk
        vec_ref[1] = kk * at
        vec_ref[2] = kh
        sa = jnp.zeros((RW_N, kt.shape[1]), F32)
        for j in range(RW_N):
            sa = sa + s_ref[j] * vec_ref[0, j:j + 1, :]
        y = jnp.zeros_like(sa)
        for j in range(RW_N):
            sj = (s_ref[j] * w_ref[t, j:j + 1, :] + sa * vec_ref[1, j:j + 1, :]
                  + vt * vec_ref[2, j:j + 1, :])
            s_ref[j] = sj
            y = y + sj * r_ref[t, j:j + 1, :]
        mu = jnp.mean(y, axis=0, keepdims=True)
        yc = y - mu
        var = jnp.mean(yc * yc, axis=0, keepdims=True)
        yn = yc * lax.rsqrt(var + RW_GN_EPS) * lnw + lnb
        bonus = jnp.sum(rt * kh * rkp, axis=0, keepdims=True) * vt
        o_ref[t] = (yn + bonus) * g_ref[t]
        return carry

    lax.fori_loop(0, tt, step, 0)


def rwkv_scan(r, k, v, w, a, g, kkp, kap, rkp, lnw, lnb, tt=16):
    t, n, lanes = r.shape
    tt = min(tt, t)
    blk = pl.BlockSpec((tt, n, lanes), lambda i: (i, 0, 0))
    par = _const_spec((n, lanes))
    return pl.pallas_call(
        functools.partial(_rwkv_scan_kernel, tt=tt),
        grid=(t // tt,),
        in_specs=[blk] * 6 + [par] * 5,
        out_specs=blk,
        out_shape=jax.ShapeDtypeStruct((t, n, lanes), F32),
        scratch_shapes=[pltpu.VMEM((n, n, lanes), F32), pltpu.VMEM((3, n, lanes), F32)],
        compiler_params=_cparams("arbitrary"),
        name="rwkv_scan",
    )(r, k, v, w, a, g, kkp, kap, rkp, lnw, lnb)


def _time_major(x, batch, seq):
    return x.reshape(batch, seq, RW_HEADS, RW_N).transpose(1, 3, 0, 2).reshape(seq, RW_N, batch * RW_HEADS)


def _token_major(x, batch, seq):
    return x.reshape(seq, RW_N, batch, RW_HEADS).transpose(2, 0, 3, 1).reshape(batch * seq, RW_W)


def _head_param(p, batch):
    return jnp.tile(p.reshape(RW_HEADS, RW_N).T, (1, batch))


def _pad_rows(w, rows):
    return jnp.pad(w, ((0, rows - w.shape[0]), (0, 0)))


def rwkv7(p_main, p_low, mix, w0, w2, a0, a2, g2, k_k, k_a, r_k, ln_w, ln_b, batch, seq):
    mix_main = mix[:3 * RW_W].reshape(1, -1)
    lo = mix[3 * RW_W:]
    c1 = RW_DECAY_RANK
    c2 = c1 + RW_A_RANK
    mix_low = jnp.concatenate([jnp.pad(lo[:c1], (0, 128 - RW_DECAY_RANK)),
                               jnp.pad(lo[c1:c2], (0, 128 - RW_A_RANK)),
                               jnp.pad(lo[c2:], (0, 256 - RW_GATE_RANK))]).reshape(1, RW_LOW_PAD)
    r, k, v, w, a, g = rwkv_prep(
        p_main, p_low, mix_main, mix_low, w0.reshape(1, -1), _pad_rows(w2, 128).astype(BF16),
        a0.reshape(1, -1), _pad_rows(a2, 128).astype(BF16), _pad_rows(g2, 256).astype(BF16), seq)
    tm = lambda x: _time_major(x, batch, seq)
    hp = lambda p: _head_param(p.reshape(-1), batch)
    o = rwkv_scan(tm(r), tm(k), tm(v), tm(w), tm(a), tm(g), hp(k_k), hp(k_a), hp(r_k), hp(ln_w), hp(ln_b))
    return _token_major(o, batch, seq)


def _ssd_kernel(z_ref, xbc_ref, dt_ref, cw_ref, cb_ref, dtb_ref, alog_ref, dexp_ref, nw_ref, ex_ref, tri_ref,
                o_ref, xpad_ref, xact_ref, st_ref, y_ref):
    c = pl.program_id(1)

    @pl.when(c == 0)
    def _():
        st_ref[...] = jnp.zeros_like(st_ref)
        xpad_ref[0:8, :] = jnp.zeros((8, SSD_CONV_DIM), F32)

    xpad_ref[8:8 + CHUNK, :] = xbc_ref[...]
    ct = 512
    for j in range(SSD_CONV_DIM // ct):
        cols = slice(j * ct, (j + 1) * ct)
        acc = cb_ref[:, cols]
        for tap in range(SSD_CONV):
            lo = 8 - (SSD_CONV - 1) + tap
            acc = acc + cw_ref[tap:tap + 1, cols] * xpad_ref[lo:lo + CHUNK, cols]
        xact_ref[:, cols] = _silu(acc)
    xpad_ref[0:8, :] = xpad_ref[CHUNK:CHUNK + 8, :]

    ex = ex_ref[...]
    dtv = _softplus(dt_ref[...] + dtb_ref[...])
    da = dtv * (-jnp.exp(alog_ref[...]))
    acum = _dot_exact_lhs(tri_ref[...], da)
    acum_t = acum.T
    dtx = _dot_exact_rhs(dtv, ex)
    acx = _dot_exact_rhs(acum, ex)
    alx = acx[CHUNK - 1:CHUNK, :]
    e_in = jnp.exp(acx)
    e_out = jnp.exp(alx - acx)
    e_last = jnp.exp(alx)
    xs = xact_ref[:, 0:SSD_INNER]
    xdt = xs * dtx
    ti = lax.broadcasted_iota(jnp.int32, (CHUNK, CHUNK), 0)
    si = lax.broadcasted_iota(jnp.int32, (CHUNK, CHUNK), 1)
    causal = ti >= si
    lane = lax.broadcasted_iota(jnp.int32, (CHUNK, 128), 1)
    nt = (((1,), (1,)), ((), ()))

    for g in range(SSD_GROUPS):
        bg = xact_ref[:, SSD_INNER + g * SSD_N:SSD_INNER + (g + 1) * SSD_N]
        cg = xact_ref[:, SSD_INNER + SSD_BC + g * SSD_N:SSD_INNER + SSD_BC + (g + 1) * SSD_N]
        cgb = cg.astype(BF16)
        cb = lax.dot_general(cgb, bg.astype(BF16), nt, preferred_element_type=F32)
        gc = slice(g * SSD_GW, (g + 1) * SSD_GW)
        sgt = st_ref[g]
        y_g = _dot(cgb, sgt.astype(BF16)) * e_in[:, gc]
        xdt_g = xdt[:, gc]
        for m in range(SSD_HPG // 2):
            xp = xdt_g[:, m * 128:(m + 1) * 128]
            yp = y_g[:, m * 128:(m + 1) * 128]
            for half in range(2):
                h = g * SSD_HPG + 2 * m + half
                seg = acum[:, h:h + 1] - acum_t[h:h + 1, :]
                lmat = jnp.where(causal, jnp.exp(jnp.minimum(seg, 0.0)), 0.0)
                rhs = jnp.where((lane >= 64) == (half == 1), xp, 0.0)
                yp = yp + _dot((cb * lmat).astype(BF16), rhs.astype(BF16))
            y_ref[:, g * SSD_GW + m * 128:g * SSD_GW + (m + 1) * 128] = yp
        st_ref[g] = sgt * e_last[:, gc] + _dot(bg.T.astype(BF16), (xdt_g * e_out[:, gc]).astype(BF16))

    for g in range(SSD_GROUPS):
        gc = slice(g * SSD_GW, (g + 1) * SSD_GW)
        y = y_ref[:, gc] + dexp_ref[:, gc] * xact_ref[:, gc]
        y = y * _silu(z_ref[:, gc].astype(F32))
        ms = jnp.mean(y * y, axis=-1, keepdims=True)
        o_ref[:, gc] = (y * lax.rsqrt(ms + NORM_EPS) * nw_ref[:, gc]).astype(o_ref.dtype)


def ssd(z, xbc, dt, conv_w, conv_b, dt_bias, a_log, d_skip, norm_w, batch, seq):
    nc = seq // CHUNK
    pad = lambda v: jnp.pad(v.reshape(1, -1), ((0, 0), (0, 128 - SSD_HEADS)))
    ex = np.zeros((128, SSD_INNER), np.float32)
    for h in range(SSD_HEADS):
        ex[h, h * SSD_P:(h + 1) * SSD_P] = 1.0
    tri = np.tril(np.ones((CHUNK, CHUNK), np.float32))
    row = lambda w: pl.BlockSpec((CHUNK, w), lambda b, c: (b * nc + c, 0))
    return pl.pallas_call(
        _ssd_kernel,
        grid=(batch, nc),
        in_specs=[row(SSD_INNER), row(SSD_CONV_DIM), row(128),
                  _const_spec((SSD_CONV, SSD_CONV_DIM)), _const_spec((1, SSD_CONV_DIM)),
                  _const_spec((1, 128)), _const_spec((1, 128)),
                  _const_spec((1, SSD_INNER)), _const_spec((1, SSD_INNER)),
                  _const_spec((128, SSD_INNER)), _const_spec((CHUNK, CHUNK))],
        out_specs=row(SSD_INNER),
        out_shape=jax.ShapeDtypeStruct((batch * seq, SSD_INNER), BF16),
        scratch_shapes=[pltpu.VMEM((CHUNK + 8, SSD_CONV_DIM), F32),
                        pltpu.VMEM((CHUNK, SSD_CONV_DIM), F32),
                        pltpu.VMEM((SSD_GROUPS, SSD_N, SSD_GW), F32),
                        pltpu.VMEM((CHUNK, SSD_INNER), F32)],
        compiler_params=_cparams("parallel", "arbitrary"),
        name="ssd",
    )(z, xbc, dt, conv_w, conv_b.reshape(1, -1), pad(dt_bias), pad(a_log),
      jnp.repeat(d_skip, SSD_P).reshape(1, -1), norm_w.reshape(1, -1), jnp.asarray(ex, BF16),
      jnp.asarray(tri, BF16))


FFN_TM = 256


def _router_kernel(x_ref, rt_ref, bias_ref, up_ref, ls_ref, eid_o, rank_o, gate_o, cnt_o, carry_ref, *, tb):
    @pl.when(pl.program_id(0) == 0)
    def _():
        carry_ref[...] = jnp.zeros_like(carry_ref)

    nt = (((1,), (1,)), ((), ()))
    x = x_ref[...]
    rt = rt_ref[...]
    xh = x.astype(BF16)
    xl = (x - xh.astype(F32)).astype(BF16)
    rh = rt.astype(BF16)
    rl = (rt - rh.astype(F32)).astype(BF16)
    dg = lambda a, b: lax.dot_general(a, b, nt, preferred_element_type=F32)
    logits = dg(rh, xh) + dg(rh, xl) + dg(rl, xh)
    scores = _sigmoid(logits)
    reps = tb // 128
    wide = lambda a: jnp.concatenate([a] * reps, axis=1) if reps > 1 else a
    biased = scores + wide(bias_ref[...])
    neg = -jnp.inf

    io8 = lax.broadcasted_iota(jnp.int32, (8, tb), 0)
    blocks, gs = [], []
    for g in range(N_GROUPS):
        blk = biased[8 * g:8 * g + 8, :]
        m1 = jnp.max(blk, axis=0, keepdims=True)
        first = jnp.min(jnp.where(blk == m1, io8, 8), axis=0, keepdims=True)
        m2 = jnp.max(jnp.where(io8 == first, neg, blk), axis=0, keepdims=True)
        blocks.append(blk)
        gs.append(m1 + m2)
    masked = []
    for g in range(N_GROUPS):
        ahead = jnp.zeros((1, tb), jnp.int32)
        for o in range(N_GROUPS):
            if o == g:
                continue
            beats = (gs[o] > gs[g]) | ((gs[o] == gs[g]) & (o < g))
            ahead = ahead + jnp.where(beats, 1, 0)
        masked.append(jnp.where(ahead < TOPK_GROUPS, blocks[g], neg))
    masked = jnp.concatenate(masked, axis=0)
    eidx = lax.broadcasted_iota(jnp.int32, (N_EXPERTS, tb), 0)
    ahead = jnp.zeros((N_EXPERTS, tb), jnp.int32)
    for o in range(N_EXPERTS):
        row = masked[o:o + 1, :]
        beats = (row > masked) | ((row == masked) & (eidx > o))
        ahead = ahead + jnp.where(beats, 1, 0)
    sel = ahead < TOP_K
    self_ = jnp.where(sel, 1.0, 0.0)
    wts = jnp.where(sel, scores, 0.0)
    gate = wts / jnp.sum(wts, axis=0, keepdims=True) * ROUTED_SCALE

    selb = self_.astype(BF16)
    carry = carry_ref[...]
    rank = _dot(selb, up_ref[...]) + wide(carry)
    new_carry = carry + _dot(selb, jnp.ones((tb, 128), BF16))
    carry_ref[...] = new_carry
    cnt_o[...] = new_carry
    before = _dot(ls_ref[...], selb)
    eidf = eidx.astype(F32)
    for j in range(TOP_K):
        hit = sel & (before == float(j))
        pick = lambda a: jnp.sum(jnp.where(hit, a, 0.0), axis=0, keepdims=True)
        eid_o[j:j + 1, :] = pick(eidf).astype(jnp.int32)
        rank_o[j:j + 1, :] = pick(rank).astype(jnp.int32)
        gate_o[j:j + 1, :] = pick(gate)


def moe_route(h32, router, bias, tb=256):
    m = h32.shape[0]
    tb = min(tb, m)
    up = np.triu(np.ones((tb, tb), np.float32), k=1)
    ls = np.tril(np.ones((N_EXPERTS, N_EXPERTS), np.float32), k=-1)
    slot = lambda dt: jax.ShapeDtypeStruct((TOP_K, m), dt)
    return pl.pallas_call(
        functools.partial(_router_kernel, tb=tb),
        grid=(m // tb,),
        in_specs=[pl.BlockSpec((tb, D_MODEL), lambda i: (i, 0)),
                  _const_spec((N_EXPERTS, D_MODEL)), _const_spec((N_EXPERTS, 128)),
                  _const_spec((tb, tb)), _const_spec((N_EXPERTS, N_EXPERTS))],
        out_specs=[pl.BlockSpec((TOP_K, tb), lambda i: (0, i))] * 3
                  + [pl.BlockSpec((N_EXPERTS, 128), lambda i: (0, 0))],
        out_shape=[slot(jnp.int32), slot(jnp.int32), slot(F32),
                   jax.ShapeDtypeStruct((N_EXPERTS, 128), F32)],
        scratch_shapes=[pltpu.VMEM((N_EXPERTS, 128), F32)],
        compiler_params=_cparams("arbitrary"),
        name="moe_route",
    )(h32, router.T, jnp.broadcast_to(bias.reshape(N_EXPERTS, 1), (N_EXPERTS, 128)),
      jnp.asarray(up, BF16), jnp.asarray(ls, BF16))


def _row_copy(src, src_row, dst, dst_row, sem):
    return pltpu.make_async_copy(src.at[pl.ds(src_row, 1), :], dst.at[pl.ds(dst_row, 1), :], sem)


def _dispatch_kernel(pos_ref, x_hbm, xs_in, xs_out, sem, *, tb):
    del xs_in
    base = pl.program_id(0) * tb

    def issue(t, c):
        for j in range(TOP_K):
            _row_copy(x_hbm, base + t, xs_out, pos_ref[j, t], sem).start()
        return c

    lax.fori_loop(0, tb, issue, 0)

    def drain(t, c):
        for j in range(TOP_K):
            _row_copy(x_hbm, 0, xs_out, 0, sem).wait()
        return c

    lax.fori_loop(0, tb, drain, 0)


def moe_dispatch(xpk, pos8, n_rows, tb=512):
    m, w = xpk.shape
    tb = min(tb, m)
    zeros = jnp.zeros((n_rows, w), xpk.dtype)
    return pl.pallas_call(
        functools.partial(_dispatch_kernel, tb=tb),
        grid=(m // tb,),
        in_specs=[pl.BlockSpec((TOP_K, tb), lambda i: (0, i), memory_space=pltpu.SMEM),
                  pl.BlockSpec(memory_space=pl.ANY), pl.BlockSpec(memory_space=pl.ANY)],
        out_specs=pl.BlockSpec(memory_space=pl.ANY),
        out_shape=jax.ShapeDtypeStruct((n_rows, w), xpk.dtype),
        scratch_shapes=[pltpu.SemaphoreType.DMA(())],
        input_output_aliases={2: 0},
        compiler_params=_cparams("arbitrary"),
        name="moe_dispatch",
    )(pos8, xpk, zeros)


def _ffn_kernel(te_ref, nu_ref, xs_ref, wg_ref, wu_ref, wd_ref, ys_ref):
    del te_ref

    @pl.when(pl.program_id(0) < nu_ref[0])
    def _():
        lo, hi = _unpack_halves(xs_ref[...])
        lo = lo.astype(BF16)
        hi = hi.astype(BF16)
        hg = _dot(lo, wg_ref[0, :HALF, :]) + _dot(hi, wg_ref[0, HALF:, :])
        hu = _dot(lo, wu_ref[0, :HALF, :]) + _dot(hi, wu_ref[0, HALF:, :])
        y = _dot((_silu(hg) * hu).astype(BF16), wd_ref[0])
        ys_ref[...] = _pack_halves(y[:, :HALF], y[:, HALF:])


def moe_experts(xs, tile_expert, n_used, w_gate, w_up, w_down):
    n_rows, w = xs.shape
    n_tiles = n_rows // FFN_TM
    wspec = lambda shp: pl.BlockSpec((1,) + shp, lambda i, te, nu: (te[i], 0, 0))
    grid_spec = pltpu.PrefetchScalarGridSpec(
        num_scalar_prefetch=2,
        grid=(n_tiles,),
        in_specs=[pl.BlockSpec((FFN_TM, w), lambda i, te, nu: (i, 0)),
                  wspec((D_MODEL, EXPERT_DIM)), wspec((D_MODEL, EXPERT_DIM)), wspec((EXPERT_DIM, D_MODEL))],
        out_specs=pl.BlockSpec((FFN_TM, w), lambda i, te, nu: (i, 0)),
    )
    return pl.pallas_call(
        _ffn_kernel,
        grid_spec=grid_spec,
        out_shape=jax.ShapeDtypeStruct((n_rows, w), xs.dtype),
        compiler_params=_cparams("arbitrary"),
        name="moe_experts",
    )(tile_expert, n_used, xs, w_gate, w_up, w_down)


def _combine_kernel(pos_ref, gate_ref, h32_ref, h16_ref, sg_ref, su_ref, sd_ref, lng_ref, lnb_ref, ys_hbm,
                    o32_ref, o16_ref, buf_ref, sem, *, tb):
    def issue(t, c):
        for j in range(TOP_K):
            pltpu.make_async_copy(ys_hbm.at[pl.ds(pos_ref[j, t], 1), :],
                                  buf_ref.at[j, pl.ds(t, 1), :], sem).start()
        return c

    lax.fori_loop(0, tb, issue, 0)

    x = h16_ref[...]
    hdn = (_silu(_dot(x, sg_ref[...])) * _dot(x, su_ref[...])).astype(BF16)
    shared = _dot(hdn, sd_ref[...])
    acc_lo = shared[:, :HALF] + DN_ALPHA * h32_ref[:, :HALF]
    acc_hi = shared[:, HALF:] + DN_ALPHA * h32_ref[:, HALF:]

    def drain(t, c):
        for j in range(TOP_K):
            pltpu.make_async_copy(ys_hbm.at[pl.ds(0, 1), :], buf_ref.at[0, pl.ds(0, 1), :], sem).wait()
        return c

    lax.fori_loop(0, tb, drain, 0)

    for j in range(TOP_K):
        lo, hi = _unpack_halves(buf_ref[j])
        gj = gate_ref[:, j:j + 1]
        acc_lo = acc_lo + gj * lo
        acc_hi = acc_hi + gj * hi
    mu = (jnp.sum(acc_lo, axis=-1, keepdims=True) + jnp.sum(acc_hi, axis=-1, keepdims=True)) / D_MODEL
    c_lo = acc_lo - mu
    c_hi = acc_hi - mu
    var = (jnp.sum(c_lo * c_lo, axis=-1, keepdims=True) + jnp.sum(c_hi * c_hi, axis=-1, keepdims=True)) / D_MODEL
    inv = lax.rsqrt(var + NORM_EPS)
    o_lo = c_lo * inv * lng_ref[:, :HALF] + lnb_ref[:, :HALF]
    o_hi = c_hi * inv * lng_ref[:, HALF:] + lnb_ref[:, HALF:]
    o32_ref[:, :HALF] = o_lo
    o32_ref[:, HALF:] = o_hi
    o16_ref[:, :HALF] = o_lo.astype(BF16)
    o16_ref[:, HALF:] = o_hi.astype(BF16)


def moe_combine(ys, pos8, gate_t, h32, h16, s_gate, s_up, s_down, ln_g, ln_b, tb=128):
    m, d = h32.shape
    tb = min(tb, m)
    row = lambda w: pl.BlockSpec((tb, w), lambda i: (i, 0))
    return pl.pallas_call(
        functools.partial(_combine_kernel, tb=tb),
        grid=(m // tb,),
        in_specs=[pl.BlockSpec((TOP_K, tb), lambda i: (0, i), memory_space=pltpu.SMEM),
                  row(TOP_K), row(d), row(d),
                  _const_spec(s_gate.shape), _const_spec(s_up.shape), _const_spec(s_down.shape),
                  _const_spec((1, d)), _const_spec((1, d)),
                  pl.BlockSpec(memory_space=pl.ANY)],
        out_specs=[row(d), row(d)],
        out_shape=[jax.ShapeDtypeStruct((m, d), F32), jax.ShapeDtypeStruct((m, d), BF16)],
        scratch_shapes=[pltpu.VMEM((TOP_K, tb, d // 2), jnp.uint32), pltpu.SemaphoreType.DMA(())],
        compiler_params=_cparams("arbitrary"),
        name="moe_combine",
    )(pos8, gate_t, h32, h16, s_gate, s_up, s_down, ln_g.reshape(1, d), ln_b.reshape(1, d), ys)


def moe_layer(h32, h16, hpk, router, bias, w_gate, w_up, w_down, s_gate, s_up, s_down, ln_g, ln_b):
    m = h32.shape[0]
    eid8, rank8, gate8, counts = moe_route(h32, router, bias)
    cnt = counts[:, 0].astype(jnp.int32)
    seg = ((cnt + FFN_TM - 1) // FFN_TM) * FFN_TM
    ends = jnp.cumsum(seg)
    offs = ends - seg
    n_rows = m * TOP_K + N_EXPERTS * FFN_TM
    n_tiles = n_rows // FFN_TM
    pos8 = offs[eid8] + rank8
    tile_start = jnp.arange(n_tiles, dtype=jnp.int32) * FFN_TM
    tile_expert = jnp.minimum(jnp.searchsorted(ends, tile_start, side="right"), N_EXPERTS - 1).astype(jnp.int32)
    n_used = (ends[-1:] // FFN_TM).astype(jnp.int32)
    xs = moe_dispatch(hpk, pos8, n_rows)
    ys = moe_experts(xs, tile_expert, n_used, w_gate.astype(BF16), w_up.astype(BF16), w_down.astype(BF16))
    return moe_combine(ys, pos8, gate8.T, h32, h16, s_gate.astype(BF16), s_up.astype(BF16),
                       s_down.astype(BF16), ln_g, ln_b)


def _even_mixer(h16, lb, in_proj, out_proj, hg_gn_w, rw_mix, rw_w0, rw_w2, rw_a0, rw_a2, rw_g2, rw_kk, rw_ka,
                rw_rk, rw_ln_w, rw_ln_b, batch, seq):
    hg_cols = 4 * HG_W
    main_end = hg_cols + 3 * RW_W
    c1 = main_end + RW_DECAY_RANK
    c2 = c1 + RW_A_RANK
    padc = lambda w, n: jnp.pad(w, ((0, 0), (0, n - w.shape[1])))
    w_low = jnp.concatenate([padc(in_proj[:, main_end:c1], 128), padc(in_proj[:, c1:c2], 128),
                             padc(in_proj[:, c2:], 256)], axis=1)
    p_hg = matmul(h16, in_proj[:, :hg_cols].astype(BF16), F32)
    p_main = matmul(h16, in_proj[:, hg_cols:main_end].astype(BF16), F32)
    p_low = matmul(h16, w_low.astype(BF16), F32)
    o_a = hgrn2(p_hg, lb, hg_gn_w, batch, seq)
    o_b = rwkv7(p_main, p_low, rw_mix, rw_w0, rw_w2, rw_a0, rw_a2, rw_g2, rw_kk, rw_ka, rw_rk, rw_ln_w,
                rw_ln_b, batch, seq)
    return jnp.concatenate([o_a, o_b.astype(BF16)], axis=-1)


def _odd_mixer(h16, in_proj, conv_w, conv_b, dt_bias, a_log, d_skip, norm_w, batch, seq):
    c1 = SSD_INNER
    c2 = c1 + SSD_CONV_DIM
    z = matmul(h16, in_proj[:, :c1].astype(BF16), BF16)
    xbc = matmul(h16, in_proj[:, c1:c2].astype(BF16), F32)
    dt = matmul(h16, jnp.pad(in_proj[:, c2:], ((0, 0), (0, 128 - SSD_HEADS))).astype(BF16), F32)
    return ssd(z, xbc, dt, conv_w, conv_b, dt_bias, a_log, d_skip, norm_w, batch, seq)


def kernel(x, hg_lb, ev_in_proj, ev_out_proj, hg_gn_w, rw_mix, rw_w0, rw_w2, rw_a0, rw_a2, rw_g2, rw_kk, rw_ka,
           rw_rk, rw_ln_w, rw_ln_b, od_in_proj, od_conv_w, od_conv_b, od_dt_bias, od_a_log, od_d, od_norm_w,
           od_out_proj, moe_router, moe_bias, moe_w_gate, moe_w_up, moe_w_down, sh_w_gate, sh_w_up, sh_w_down,
           ln1_g, ln1_b, ln2_g, ln2_b):
    batch, seq, d = x.shape
    lbs = jnp.cumsum(jax.nn.softmax(hg_lb.astype(F32), axis=0), axis=0)
    h32 = x.reshape(batch * seq, d)
    h16 = h32.astype(BF16)
    for l in range(DEPTH):
        if l % 2 == 0:
            e = l // 2
            mix = _even_mixer(h16, lbs[l], ev_in_proj[e], ev_out_proj[e], hg_gn_w[e], rw_mix[e], rw_w0[e],
                              rw_w2[e], rw_a0[e], rw_a2[e], rw_g2[e], rw_kk[e], rw_ka[e], rw_rk[e],
                              rw_ln_w[e], rw_ln_b[e], batch, seq)
            w_out = ev_out_proj[e]
        else:
            o = l // 2
            mix = _odd_mixer(h16, od_in_proj[o], od_conv_w[o], od_conv_b[o], od_dt_bias[o], od_a_log[o],
                             od_d[o], od_norm_w[o], batch, seq)
            w_out = od_out_proj[o]
        h32, h16, hpk = matmul_residual_ln(mix, w_out.astype(BF16), h32, ln1_g[l], ln1_b[l])
        h32, h16 = moe_layer(h32, h16, hpk, moe_router[l], moe_bias[l], moe_w_gate[l], moe_w_up[l],
                             moe_w_down[l], sh_w_gate[l], sh_w_up[l], sh_w_down[l], ln2_g[l], ln2_b[l])
    return h32.reshape(batch, seq, d)
```

```python
import functools
import math

import jax
import jax.numpy as jnp
import numpy as np
from jax import lax
from jax.experimental import pallas as pl
from jax.experimental.pallas import tpu as pltpu

F32 = jnp.float32
BF16 = jnp.bfloat16

D_MODEL = 2048
DEPTH = 2
CHUNK = 64

HG_HEADS = 8
HG_D = 128
HG_W = HG_HEADS * HG_D
RW_HEADS = 16
RW_N = 64
RW_W = RW_HEADS * RW_N
RW_DECAY_RANK = 64
RW_A_RANK = 64
RW_GATE_RANK = 160
RW_GN_EPS = 64e-5
RW_LOW_PAD = 512

SSD_INNER = 2 * D_MODEL
SSD_P = 64
SSD_HEADS = SSD_INNER // SSD_P
SSD_GROUPS = 8
SSD_HPG = SSD_HEADS // SSD_GROUPS
SSD_N = 128
SSD_CONV = 4
SSD_BC = SSD_GROUPS * SSD_N
SSD_CONV_DIM = SSD_INNER + 2 * SSD_BC
SSD_GW = SSD_HPG * SSD_P

N_EXPERTS = 64
TOP_K = 8
N_GROUPS = 8
TOPK_GROUPS = 4
EXPERT_DIM = 512
ROUTED_SCALE = 2.5

NORM_EPS = 1e-5
DN_ALPHA = (2 * DEPTH) ** 0.25

VMEM_LIMIT = 56 * 1024 * 1024


def _cparams(*sem):
    return pltpu.CompilerParams(dimension_semantics=sem, vmem_limit_bytes=VMEM_LIMIT)


def _const_spec(shape):
    nd = len(shape)
    return pl.BlockSpec(shape, lambda *_: (0,) * nd, pipeline_mode=pl.Buffered(1))


def _split3(x):
    hi = x.astype(BF16)
    r1 = x - hi.astype(F32)
    mid = r1.astype(BF16)
    lo = (r1 - mid.astype(F32)).astype(BF16)
    return hi, mid, lo


def _dot(a, b):
    return jnp.dot(a, b, preferred_element_type=F32)


def _dot_exact_lhs(m_bf16, x):
    hi, mid, lo = _split3(x)
    return _dot(m_bf16, hi) + _dot(m_bf16, mid) + _dot(m_bf16, lo)


def _dot_exact_rhs(x, m_bf16):
    hi, mid, lo = _split3(x)
    return _dot(hi, m_bf16) + _dot(mid, m_bf16) + _dot(lo, m_bf16)


def _sigmoid(x):
    return 1.0 / (1.0 + jnp.exp(-x))


def _silu(x):
    return x * _sigmoid(x)


def _softplus(x):
    return jnp.maximum(x, 0.0) + jnp.log(1.0 + jnp.exp(-jnp.abs(x)))


def _mm_kernel(x_ref, w_ref, o_ref):
    o_ref[...] = _dot(x_ref[...], w_ref[...]).astype(o_ref.dtype)


def matmul(x, w, out_dtype, tm=1024, tn=512):
    m, k = x.shape
    n = w.shape[1]
    tm = min(tm, m)
    tn = min(tn, n)
    assert m % tm == 0 and n % tn == 0
    return pl.pallas_call(
        _mm_kernel,
        grid=(m // tm, n // tn),
        in_specs=[pl.BlockSpec((tm, k), lambda i, j: (i, 0)),
                  pl.BlockSpec((k, tn), lambda i, j: (0, j))],
        out_specs=pl.BlockSpec((tm, tn), lambda i, j: (i, j)),
        out_shape=jax.ShapeDtypeStruct((m, n), out_dtype),
        compiler_params=_cparams("parallel", "arbitrary"),
        name="matmul",
    )(x, w)


def _layer_norm_rows(y, g, b):
    mu = jnp.mean(y, axis=-1, keepdims=True)
    yc = y - mu
    var = jnp.mean(yc * yc, axis=-1, keepdims=True)
    return yc * lax.rsqrt(var + NORM_EPS) * g + b


def _bf16_bits(x):
    u = lax.bitcast_convert_type(x, jnp.uint32)
    u = u + jnp.uint32(0x7FFF) + ((u >> 16) & jnp.uint32(1))
    return u & jnp.uint32(0xFFFF0000)


def _pack_halves(lo, hi):
    return (_bf16_bits(lo) >> 16) | _bf16_bits(hi)


def _unpack_halves(u):
    lo = lax.bitcast_convert_type(u << 16, F32)
    hi = lax.bitcast_convert_type(u & jnp.uint32(0xFFFF0000), F32)
    return lo, hi


HALF = D_MODEL // 2


def _mm_ln_kernel(x_ref, w_ref, h_ref, g_ref, b_ref, o32_ref, o16_ref, opk_ref):
    y = _dot(x_ref[...], w_ref[...]) + DN_ALPHA * h_ref[...]
    o = _layer_norm_rows(y, g_ref[...], b_ref[...])
    o32_ref[...] = o
    o16_ref[...] = o.astype(BF16)
    opk_ref[...] = _pack_halves(o[:, :HALF], o[:, HALF:])


def matmul_residual_ln(x, w, h, g, b, tm=256):
    m, k = x.shape
    d = w.shape[1]
    tm = min(tm, m)
    assert m % tm == 0
    return pl.pallas_call(
        _mm_ln_kernel,
        grid=(m // tm,),
        in_specs=[pl.BlockSpec((tm, k), lambda i: (i, 0)),
                  _const_spec((k, d)),
                  pl.BlockSpec((tm, d), lambda i: (i, 0)),
                  _const_spec((1, d)),
                  _const_spec((1, d))],
        out_specs=[pl.BlockSpec((tm, d), lambda i: (i, 0)),
                   pl.BlockSpec((tm, d), lambda i: (i, 0)),
                   pl.BlockSpec((tm, d // 2), lambda i: (i, 0))],
        out_shape=[jax.ShapeDtypeStruct((m, d), F32), jax.ShapeDtypeStruct((m, d), BF16),
                   jax.ShapeDtypeStruct((m, d // 2), jnp.uint32)],
        compiler_params=_cparams("parallel"),
        name="matmul_residual_ln",
    )(x, w, h, g.reshape(1, d), b.reshape(1, d))


HG_LEVELS = (32, 16, 8, 4, 2, 1)


def _hgrn2_constants():
    t = np.arange(CHUNK)[:, None]
    u = np.arange(CHUNK)[None, :]
    blocks = [(u <= t)]
    for b in HG_LEVELS:
        start = (t // b) * b
        end = start + b - 1
        blocks.append((u >= start) & (u <= t))
        blocks.append((u > t) & (u <= end))
    blocks.append(u > t)
    cm = np.concatenate(blocks, axis=0).astype(np.float32)
    s = u
    masks = [((t // (2 * b)) == (s // (2 * b))) & ((t // b) % 2 == 1) & ((s // b) % 2 == 0) for b in HG_LEVELS]
    return cm, np.stack(masks).astype(np.float32)


def _hgrn2_kernel(q_ref, f_ref, i_ref, g_ref, lb_ref, gnw_ref, cm_ref, mask_ref, o_ref, st_ref, *, n_chunks):
    @pl.when(pl.program_id(2) == 0)
    def _():
        st_ref[...] = jnp.zeros_like(st_ref)

    lb = lb_ref[...]
    gnw = gnw_ref[...]
    cm = cm_ref[...]
    nt = (((1,), (1,)), ((), ()))

    def chunk(c, carry):
        rows = pl.ds(pl.multiple_of(c * CHUNK, CHUNK), CHUNK)
        q = q_ref[rows, :]
        fp = f_ref[rows, :]
        v = i_ref[rows, :]
        g = g_ref[rows, :]
        lf = jnp.log(lb + (1.0 - lb) * _sigmoid(fp))
        k = (1.0 - lb) * _sigmoid(-fp)
        seg = _dot_exact_lhs(cm, lf)
        bc = seg[0:CHUNK]
        st = st_ref[...]
        o = lax.dot_general((q * jnp.exp(bc)).astype(BF16), st.astype(BF16), nt, preferred_element_type=F32)
        att = jnp.zeros((CHUNK, CHUNK), F32)
        for li in range(len(HG_LEVELS)):
            base = CHUNK + 2 * CHUNK * li
            qs = q * jnp.exp(seg[base:base + CHUNK])
            ks = k * jnp.exp(seg[base + CHUNK:base + 2 * CHUNK])
            a_l = lax.dot_general(qs.astype(BF16), ks.astype(BF16), nt, preferred_element_type=F32)
            att = att + a_l * mask_ref[li]
        vb = v.astype(BF16)
        o = o + _dot(att.astype(BF16), vb) + jnp.sum(q * k, axis=-1, keepdims=True) * v
        tail = CHUNK + 2 * CHUNK * len(HG_LEVELS)
        kd = k * jnp.exp(seg[tail:tail + CHUNK])
        st_ref[...] = st * jnp.exp(bc[CHUNK - 1:CHUNK]) + _dot(v.T.astype(BF16), kd.astype(BF16))
        ms = jnp.mean(o * o, axis=-1, keepdims=True)
        o_ref[rows, :] = (o * lax.rsqrt(ms + NORM_EPS) * gnw * _silu(g)).astype(o_ref.dtype)
        return carry

    lax.fori_loop(0, n_chunks, chunk, 0)


def hgrn2(p_hg, lb, gn_w, batch, seq, tb=256):
    tb = min(tb, seq)
    nt = seq // tb
    cm, masks = _hgrn2_constants()
    spec = lambda off: pl.BlockSpec((tb, HG_D), lambda b, h, t: (b * nt + t, off + h))
    return pl.pallas_call(
        functools.partial(_hgrn2_kernel, n_chunks=tb // CHUNK),
        grid=(batch, HG_HEADS, nt),
        in_specs=[spec(0), spec(HG_HEADS), spec(2 * HG_HEADS), spec(3 * HG_HEADS),
                  pl.BlockSpec((1, HG_D), lambda b, h, t: (0, h)),
                  pl.BlockSpec((1, HG_D), lambda b, h, t: (0, h)),
                  _const_spec(cm.shape), _const_spec(masks.shape)],
        out_specs=pl.BlockSpec((tb, HG_D), lambda b, h, t: (b * nt + t, h)),
        out_shape=jax.ShapeDtypeStruct((batch * seq, HG_W), BF16),
        scratch_shapes=[pltpu.VMEM((HG_D, HG_D), F32)],
        compiler_params=_cparams("parallel", "parallel", "arbitrary"),
        name="hgrn2",
    )(p_hg, p_hg, p_hg, p_hg, lb.reshape(1, HG_W), gn_w.reshape(1, HG_W),
      jnp.asarray(cm, BF16), jnp.asarray(masks))


def _rwkv_prep_kernel(pm_ref, pmprev_ref, plo_ref, ploprev_ref, mixm_ref, mixl_ref, w0_ref, w2_ref, a0_ref,
                      a2_ref, g2_ref, r_o, k_o, v_o, w_o, a_o, g_o, *, tb, seq):
    first = (pl.program_id(0) * tb) % seq == 0
    keep = jnp.where(first, 0.0, 1.0)

    def shift_mix(cur, prev_blk, mix):
        row = lax.broadcasted_iota(jnp.int32, cur.shape, 0)
        shifted = jnp.where(row == 0, prev_blk[7:8, :] * keep, pltpu.roll(cur, 1, 0))
        return cur + (shifted - cur) * mix

    for idx, out in enumerate((r_o, k_o, v_o)):
        cols = slice(idx * RW_W, (idx + 1) * RW_W)
        out[...] = shift_mix(pm_ref[:, cols], pmprev_ref[:, cols], mixm_ref[:, cols])
    low = shift_mix(plo_ref[...], ploprev_ref[...], mixl_ref[...])
    wl = jnp.tanh(low[:, 0:128]).astype(BF16)
    al = low[:, 128:256].astype(BF16)
    gl = _sigmoid(low[:, 256:512]).astype(BF16)
    w_log = -_softplus(-(w0_ref[...] + _dot(wl, w2_ref[...]))) - 0.5
    w_o[...] = jnp.exp(-jnp.exp(w_log))
    a_o[...] = _sigmoid(a0_ref[...] + _dot(al, a2_ref[...]))
    g_o[...] = _dot(gl, g2_ref[...])


def rwkv_prep(p_main, p_low, mix_main, mix_low, w0, w2p, a0, a2p, g2p, seq, tb=256):
    m = p_main.shape[0]
    tb = min(tb, seq)
    prev = lambda i: (jnp.maximum(i * (tb // 8) - 1, 0), 0)
    out = jax.ShapeDtypeStruct((m, RW_W), F32)
    ospec = pl.BlockSpec((tb, RW_W), lambda i: (i, 0))
    return pl.pallas_call(
        functools.partial(_rwkv_prep_kernel, tb=tb, seq=seq),
        grid=(m // tb,),
        in_specs=[pl.BlockSpec((tb, 3 * RW_W), lambda i: (i, 0)),
                  pl.BlockSpec((8, 3 * RW_W), prev),
                  pl.BlockSpec((tb, RW_LOW_PAD), lambda i: (i, 0)),
                  pl.BlockSpec((8, RW_LOW_PAD), prev),
                  _const_spec((1, 3 * RW_W)), _const_spec((1, RW_LOW_PAD)),
                  _const_spec((1, RW_W)), _const_spec((128, RW_W)),
                  _const_spec((1, RW_W)), _const_spec((128, RW_W)),
                  _const_spec((256, RW_W))],
        out_specs=[ospec] * 6,
        out_shape=[out] * 6,
        compiler_params=_cparams("parallel"),
        name="rwkv_prep",
    )(p_main, p_main, p_low, p_low, mix_main, mix_low, w0, w2p, a0, a2p, g2p)


def _rwkv_scan_kernel(r_ref, k_ref, v_ref, w_ref, a_ref, g_ref, kkp_ref, kap_ref, rkp_ref, lnw_ref, lnb_ref,
                      o_ref, s_ref, vec_ref, *, tt):
    @pl.when(pl.program_id(0) == 0)
    def _():
        s_ref[...] = jnp.zeros_like(s_ref)

    kkp = kkp_ref[...]
    kap = kap_ref[...]
    rkp = rkp_ref[...]
    lnw = lnw_ref[...]
    lnb = lnb_ref[...]

    def step(t, carry):
        kt = k_ref[t]
        at = a_ref[t]
        rt = r_ref[t]
        vt = v_ref[t]
        kk = kt * kkp
        nrm = jnp.sqrt(jnp.sum(kk * kk, axis=0, keepdims=True))
        kk = kk / jnp.maximum(nrm, 1e-12)
        kh = kt * (1.0 + (at - 1.0) * kap)
        vec_ref[0] = -kk
        vec_ref[1] = kk * at
        vec_ref[2] = kh
        sa = jnp.zeros((RW_N, kt.shape[1]), F32)
        for j in range(RW_N):
            sa = sa + s_ref[j] * vec_ref[0, j:j + 1, :]
        y = jnp.zeros_like(sa)
        for j in range(RW_N):
            sj = (s_ref[j] * w_ref[t, j:j + 1, :] + sa * vec_ref[1, j:j + 1, :]
                  + vt * vec_ref[2, j:j + 1, :])
            s_ref[j] = sj
            y = y + sj * r_ref[t, j:j + 1, :]
        mu = jnp.mean(y, axis=0, keepdims=True)
        yc = y - mu
        var = jnp.mean(yc * yc, axis=0, keepdims=True)
        yn = yc * lax.rsqrt(var + RW_GN_EPS) * lnw + lnb
        bonus = jnp.sum(rt * kh * rkp, axis=0, keepdims=True) * vt
        o_ref[t] = (yn + bonus) * g_ref[t]
        return carry

    lax.fori_loop(0, tt, step, 0)


def rwkv_scan(r, k, v, w, a, g, kkp, kap, rkp, lnw, lnb, tt=16):
    t, n, lanes = r.shape
    tt = min(tt, t)
    blk = pl.BlockSpec((tt, n, lanes), lambda i: (i, 0, 0))
    par = _const_spec((n, lanes))
    return pl.pallas_call(
        functools.partial(_rwkv_scan_kernel, tt=tt),
        grid=(t // tt,),
        in_specs=[blk] * 6 + [par] * 5,
        out_specs=blk,
        out_shape=jax.ShapeDtypeStruct((t, n, lanes), F32),
        scratch_shapes=[pltpu.VMEM((n, n, lanes), F32), pltpu.VMEM((3, n, lanes), F32)],
        compiler_params=_cparams("arbitrary"),
        name="rwkv_scan",
    )(r, k, v, w, a, g, kkp, kap, rkp, lnw, lnb)


def _time_major(x, batch, seq):
    return x.reshape(batch, seq, RW_HEADS, RW_N).transpose(1, 3, 0, 2).reshape(seq, RW_N, batch * RW_HEADS)


def _token_major(x, batch, seq):
    return x.reshape(seq, RW_N, batch, RW_HEADS).transpose(2, 0, 3, 1).reshape(batch * seq, RW_W)


def _head_param(p, batch):
    return jnp.tile(p.reshape(RW_HEADS, RW_N).T, (1, batch))


def _pad_rows(w, rows):
    return jnp.pad(w, ((0, rows - w.shape[0]), (0, 0)))


def rwkv7(p_main, p_low, mix, w0, w2, a0, a2, g2, k_k, k_a, r_k, ln_w, ln_b, batch, seq):
    mix_main = mix[:3 * RW_W].reshape(1, -1)
    lo = mix[3 * RW_W:]
    c1 = RW_DECAY_RANK
    c2 = c1 + RW_A_RANK
    mix_low = jnp.concatenate([jnp.pad(lo[:c1], (0, 128 - RW_DECAY_RANK)),
                               jnp.pad(lo[c1:c2], (0, 128 - RW_A_RANK)),
                               jnp.pad(lo[c2:], (0, 256 - RW_GATE_RANK))]).reshape(1, RW_LOW_PAD)
    r, k, v, w, a, g = rwkv_prep(
        p_main, p_low, mix_main, mix_low, w0.reshape(1, -1), _pad_rows(w2, 128).astype(BF16),
        a0.reshape(1, -1), _pad_rows(a2, 128).astype(BF16), _pad_rows(g2, 256).astype(BF16), seq)
    tm = lambda x: _time_major(x, batch, seq)
    hp = lambda p: _head_param(p.reshape(-1), batch)
    o = rwkv_scan(tm(r), tm(k), tm(v), tm(w), tm(a), tm(g), hp(k_k), hp(k_a), hp(r_k), hp(ln_w), hp(ln_b))
    return _token_major(o, batch, seq)


def _ssd_kernel(z_ref, xbc_ref, dt_ref, cw_ref, cb_ref, dtb_ref, alog_ref, dexp_ref, nw_ref, ex_ref, tri_ref,
                o_ref, xpad_ref, xact_ref, st_ref, y_ref):
    c = pl.program_id(1)

    @pl.when(c == 0)
    def _():
        st_ref[...] = jnp.zeros_like(st_ref)
        xpad_ref[0:8, :] = jnp.zeros((8, SSD_CONV_DIM), F32)

    xpad_ref[8:8 + CHUNK, :] = xbc_ref[...]
    ct = 512
    for j in range(SSD_CONV_DIM // ct):
        cols = slice(j * ct, (j + 1) * ct)
        acc = cb_ref[:, cols]
        for tap in range(SSD_CONV):
            lo = 8 - (SSD_CONV - 1) + tap
            acc = acc + cw_ref[tap:tap + 1, cols] * xpad_ref[lo:lo + CHUNK, cols]
        xact_ref[:, cols] = _silu(acc)
    xpad_ref[0:8, :] = xpad_ref[CHUNK:CHUNK + 8, :]

    ex = ex_ref[...]
    dtv = _softplus(dt_ref[...] + dtb_ref[...])
    da = dtv * (-jnp.exp(alog_ref[...]))
    acum = _dot_exact_lhs(tri_ref[...], da)
    acum_t = acum.T
    dtx = _dot_exact_rhs(dtv, ex)
    acx = _dot_exact_rhs(acum, ex)
    alx = acx[CHUNK - 1:CHUNK, :]
    e_in = jnp.exp(acx)
    e_out = jnp.exp(alx - acx)
    e_last = jnp.exp(alx)
    xs = xact_ref[:, 0:SSD_INNER]
    xdt = xs * dtx
    ti = lax.broadcasted_iota(jnp.int32, (CHUNK, CHUNK), 0)
    si = lax.broadcasted_iota(jnp.int32, (CHUNK, CHUNK), 1)
    causal = ti >= si
    lane = lax.broadcasted_iota(jnp.int32, (CHUNK, 128), 1)
    nt = (((1,), (1,)), ((), ()))

    for g in range(SSD_GROUPS):
        bg = xact_ref[:, SSD_INNER + g * SSD_N:SSD_INNER + (g + 1) * SSD_N]
        cg = xact_ref[:, SSD_INNER + SSD_BC + g * SSD_N:SSD_INNER + SSD_BC + (g + 1) * SSD_N]
        cgb = cg.astype(BF16)
        cb = lax.dot_general(cgb, bg.astype(BF16), nt, preferred_element_type=F32)
        gc = slice(g * SSD_GW, (g + 1) * SSD_GW)
        sgt = st_ref[g]
        y_g = _dot(cgb, sgt.astype(BF16)) * e_in[:, gc]
        xdt_g = xdt[:, gc]
        for m in range(SSD_HPG // 2):
            xp = xdt_g[:, m * 128:(m + 1) * 128]
            yp = y_g[:, m * 128:(m + 1) * 128]
            for half in range(2):
                h = g * SSD_HPG + 2 * m + half
                seg = acum[:, h:h + 1] - acum_t[h:h + 1, :]
                lmat = jnp.where(causal, jnp.exp(jnp.minimum(seg, 0.0)), 0.0)
                rhs = jnp.where((lane >= 64) == (half == 1), xp, 0.0)
                yp = yp + _dot((cb * lmat).astype(BF16), rhs.astype(BF16))
            y_ref[:, g * SSD_GW + m * 128:g * SSD_GW + (m + 1) * 128] = yp
        st_ref[g] = sgt * e_last[:, gc] + _dot(bg.T.astype(BF16), (xdt_g * e_out[:, gc]).astype(BF16))

    for g in range(SSD_GROUPS):
        gc = slice(g * SSD_GW, (g + 1) * SSD_GW)
        y = y_ref[:, gc] + dexp_ref[:, gc] * xact_ref[:, gc]
        y = y * _silu(z_ref[:, gc].astype(F32))
        ms = jnp.mean(y * y, axis=-1, keepdims=True)
        o_ref[:, gc] = (y * lax.rsqrt(ms + NORM_EPS) * nw_ref[:, gc]).astype(o_ref.dtype)


def ssd(z, xbc, dt, conv_w, conv_b, dt_bias, a_log, d_skip, norm_w, batch, seq):
    nc = seq // CHUNK
    pad = lambda v: jnp.pad(v.reshape(1, -1), ((0, 0), (0, 128 - SSD_HEADS)))
    ex = np.zeros((128, SSD_INNER), np.float32)
    for h in range(SSD_HEADS):
        ex[h, h * SSD_P:(h + 1) * SSD_P] = 1.0
    tri = np.tril(np.ones((CHUNK, CHUNK), np.float32))
    row = lambda w: pl.BlockSpec((CHUNK, w), lambda b, c: (b * nc + c, 0))
    return pl.pallas_call(
        _ssd_kernel,
        grid=(batch, nc),
        in_specs=[row(SSD_INNER), row(SSD_CONV_DIM), row(128),
                  _const_spec((SSD_CONV, SSD_CONV_DIM)), _const_spec((1, SSD_CONV_DIM)),
                  _const_spec((1, 128)), _const_spec((1, 128)),
                  _const_spec((1, SSD_INNER)), _const_spec((1, SSD_INNER)),
                  _const_spec((128, SSD_INNER)), _const_spec((CHUNK, CHUNK))],
        out_specs=row(SSD_INNER),
        out_shape=jax.ShapeDtypeStruct((batch * seq, SSD_INNER), BF16),
        scratch_shapes=[pltpu.VMEM((CHUNK + 8, SSD_CONV_DIM), F32),
                        pltpu.VMEM((CHUNK, SSD_CONV_DIM), F32),
                        pltpu.VMEM((SSD_GROUPS, SSD_N, SSD_GW), F32),
                        pltpu.VMEM((CHUNK, SSD_INNER), F32)],
        compiler_params=_cparams("parallel", "arbitrary"),
        name="ssd",
    )(z, xbc, dt, conv_w, conv_b.reshape(1, -1), pad(dt_bias), pad(a_log),
      jnp.repeat(d_skip, SSD_P).reshape(1, -1), norm_w.reshape(1, -1), jnp.asarray(ex, BF16),
      jnp.asarray(tri, BF16))


FFN_TM = 256


def _router_kernel(x_ref, rt_ref, bias_ref, up_ref, ls_ref, eid_o, rank_o, gate_o, cnt_o, carry_ref, *, tb):
    @pl.when(pl.program_id(0) == 0)
    def _():
        carry_ref[...] = jnp.zeros_like(carry_ref)

    nt = (((1,), (1,)), ((), ()))
    x = x_ref[...]
    rt = rt_ref[...]
    xh = x.astype(BF16)
    xl = (x - xh.astype(F32)).astype(BF16)
    rh = rt.astype(BF16)
    rl = (rt - rh.astype(F32)).astype(BF16)
    dg = lambda a, b: lax.dot_general(a, b, nt, preferred_element_type=F32)
    logits = dg(rh, xh) + dg(rh, xl) + dg(rl, xh)
    scores = _sigmoid(logits)
    reps = tb // 128
    wide = lambda a: jnp.concatenate([a] * reps, axis=1) if reps > 1 else a
    biased = scores + wide(bias_ref[...])
    neg = -jnp.inf

    io8 = lax.broadcasted_iota(jnp.int32, (8, tb), 0)
    blocks, gs = [], []
    for g in range(N_GROUPS):
        blk = biased[8 * g:8 * g + 8, :]
        m1 = jnp.max(blk, axis=0, keepdims=True)
        first = jnp.min(jnp.where(blk == m1, io8, 8), axis=0, keepdims=True)
        m2 = jnp.max(jnp.where(io8 == first, neg, blk), axis=0, keepdims=True)
        blocks.append(blk)
        gs.append(m1 + m2)
    masked = []
    for g in range(N_GROUPS):
        ahead = jnp.zeros((1, tb), jnp.int32)
        for o in range(N_GROUPS):
            if o == g:
                continue
            beats = (gs[o] > gs[g]) | ((gs[o] == gs[g]) & (o < g))
            ahead = ahead + jnp.where(beats, 1, 0)
        masked.append(jnp.where(ahead < TOPK_GROUPS, blocks[g], neg))
    masked = jnp.concatenate(masked, axis=0)
    eidx = lax.broadcasted_iota(jnp.int32, (N_EXPERTS, tb), 0)
    ahead = jnp.zeros((N_EXPERTS, tb), jnp.int32)
    for o in range(N_EXPERTS):
        row = masked[o:o + 1, :]
        beats = (row > masked) | ((row == masked) & (eidx > o))
        ahead = ahead + jnp.where(beats, 1, 0)
    sel = ahead < TOP_K
    self_ = jnp.where(sel, 1.0, 0.0)
    wts = jnp.where(sel, scores, 0.0)
    gate = wts / jnp.sum(wts, axis=0, keepdims=True) * ROUTED_SCALE

    selb = self_.astype(BF16)
    carry = carry_ref[...]
    rank = _dot(selb, up_ref[...]) + wide(carry)
    new_carry = carry + _dot(selb, jnp.ones((tb, 128), BF16))
    carry_ref[...] = new_carry
    cnt_o[...] = new_carry
    before = _dot(ls_ref[...], selb)
    eidf = eidx.astype(F32)
    for j in range(TOP_K):
        hit = sel & (before == float(j))
        pick = lambda a: jnp.sum(jnp.where(hit, a, 0.0), axis=0, keepdims=True)
        eid_o[j:j + 1, :] = pick(eidf).astype(jnp.int32)
        rank_o[j:j + 1, :] = pick(rank).astype(jnp.int32)
        gate_o[j:j + 1, :] = pick(gate)


def moe_route(h32, router, bias, tb=256):
    m = h32.shape[0]
    tb = min(tb, m)
    up = np.triu(np.ones((tb, tb), np.float32), k=1)
    ls = np.tril(np.ones((N_EXPERTS, N_EXPERTS), np.float32), k=-1)
    slot = lambda dt: jax.ShapeDtypeStruct((TOP_K, m), dt)
    return pl.pallas_call(
        functools.partial(_router_kernel, tb=tb),
        grid=(m // tb,),
        in_specs=[pl.BlockSpec((tb, D_MODEL), lambda i: (i, 0)),
                  _const_spec((N_EXPERTS, D_MODEL)), _const_spec((N_EXPERTS, 128)),
                  _const_spec((tb, tb)), _const_spec((N_EXPERTS, N_EXPERTS))],
        out_specs=[pl.BlockSpec((TOP_K, tb), lambda i: (0, i))] * 3
                  + [pl.BlockSpec((N_EXPERTS, 128), lambda i: (0, 0))],
        out_shape=[slot(jnp.int32), slot(jnp.int32), slot(F32),
                   jax.ShapeDtypeStruct((N_EXPERTS, 128), F32)],
        scratch_shapes=[pltpu.VMEM((N_EXPERTS, 128), F32)],
        compiler_params=_cparams("arbitrary"),
        name="moe_route",
    )(h32, router.T, jnp.broadcast_to(bias.reshape(N_EXPERTS, 1), (N_EXPERTS, 128)),
      jnp.asarray(up, BF16), jnp.asarray(ls, BF16))


def _row_copy(src, src_row, dst, dst_row, sem):
    return pltpu.make_async_copy(src.at[pl.ds(src_row, 1), :], dst.at[pl.ds(dst_row, 1), :], sem)


def _dispatch_kernel(pos_ref, x_ref, xs_in, xs_out, sem, *, tb):
    del xs_in

    def issue(t, c):
        for j in range(TOP_K):
            _row_copy(x_ref, t, xs_out, pos_ref[j, t], sem).start()
        return c

    lax.fori_loop(0, tb, issue, 0)

    def drain(t, c):
        for j in range(TOP_K):
            _row_copy(x_ref, 0, xs_out, 0, sem).wait()
        return c

    lax.fori_loop(0, tb, drain, 0)


def moe_dispatch(xpk, pos8, n_rows, tb=256):
    m, w = xpk.shape
    tb = min(tb, m)
    zeros = jnp.zeros((n_rows, w), xpk.dtype)
    return pl.pallas_call(
        functools.partial(_dispatch_kernel, tb=tb),
        grid=(m // tb,),
        in_specs=[pl.BlockSpec((TOP_K, tb), lambda i: (0, i), memory_space=pltpu.SMEM),
                  pl.BlockSpec((tb, w), lambda i: (i, 0)), pl.BlockSpec(memory_space=pl.ANY)],
        out_specs=pl.BlockSpec(memory_space=pl.ANY),
        out_shape=jax.ShapeDtypeStruct((n_rows, w), xpk.dtype),
        scratch_shapes=[pltpu.SemaphoreType.DMA(())],
        input_output_aliases={2: 0},
        compiler_params=_cparams("arbitrary"),
        name="moe_dispatch",
    )(pos8, xpk, zeros)


def _ffn_kernel(te_ref, nu_ref, xs_ref, wg_ref, wu_ref, wd_ref, ys_ref):
    del te_ref

    @pl.when(pl.program_id(0) < nu_ref[0])
    def _():
        lo, hi = _unpack_halves(xs_ref[...])
        lo = lo.astype(BF16)
        hi = hi.astype(BF16)
        hg = _dot(lo, wg_ref[0, :HALF, :]) + _dot(hi, wg_ref[0, HALF:, :])
        hu = _dot(lo, wu_ref[0, :HALF, :]) + _dot(hi, wu_ref[0, HALF:, :])
        y = _dot((_silu(hg) * hu).astype(BF16), wd_ref[0])
        ys_ref[...] = _pack_halves(y[:, :HALF], y[:, HALF:])


def moe_experts(xs, tile_expert, n_used, w_gate, w_up, w_down):
    n_rows, w = xs.shape
    n_tiles = n_rows // FFN_TM
    wspec = lambda shp: pl.BlockSpec((1,) + shp, lambda i, te, nu: (te[i], 0, 0))
    grid_spec = pltpu.PrefetchScalarGridSpec(
        num_scalar_prefetch=2,
        grid=(n_tiles,),
        in_specs=[pl.BlockSpec((FFN_TM, w), lambda i, te, nu: (i, 0)),
                  wspec((D_MODEL, EXPERT_DIM)), wspec((D_MODEL, EXPERT_DIM)), wspec((EXPERT_DIM, D_MODEL))],
        out_specs=pl.BlockSpec((FFN_TM, w), lambda i, te, nu: (i, 0)),
    )
    return pl.pallas_call(
        _ffn_kernel,
        grid_spec=grid_spec,
        out_shape=jax.ShapeDtypeStruct((n_rows, w), xs.dtype),
        compiler_params=_cparams("arbitrary"),
        name="moe_experts",
    )(tile_expert, n_used, xs, w_gate, w_up, w_down)


def _combine_kernel(pos_ref, gate_ref, h32_ref, h16_ref, sg_ref, su_ref, sd_ref, lng_ref, lnb_ref, ys_hbm,
                    o32_ref, o16_ref, buf_ref, sem, *, tb):
    def issue(t, c):
        for j in range(TOP_K):
            pltpu.make_async_copy(ys_hbm.at[pl.ds(pos_ref[j, t], 1), :],
                                  buf_ref.at[j, pl.ds(t, 1), :], sem).start()
        return c

    lax.fori_loop(0, tb, issue, 0)

    x = h16_ref[...]
    hdn = (_silu(_dot(x, sg_ref[...])) * _dot(x, su_ref[...])).astype(BF16)
    shared = _dot(hdn, sd_ref[...])
    acc_lo = shared[:, :HALF] + DN_ALPHA * h32_ref[:, :HALF]
    acc_hi = shared[:, HALF:] + DN_ALPHA * h32_ref[:, HALF:]

    def drain(t, c):
        for j in range(TOP_K):
            pltpu.make_async_copy(ys_hbm.at[pl.ds(0, 1), :], buf_ref.at[0, pl.ds(0, 1), :], sem).wait()
        return c

    lax.fori_loop(0, tb, drain, 0)

    for j in range(TOP_K):
        lo, hi = _unpack_halves(buf_ref[j])
        gj = gate_ref[:, j:j + 1]
        acc_lo = acc_lo + gj * lo
        acc_hi = acc_hi + gj * hi
    mu = (jnp.sum(acc_lo, axis=-1, keepdims=True) + jnp.sum(acc_hi, axis=-1, keepdims=True)) / D_MODEL
    c_lo = acc_lo - mu
    c_hi = acc_hi - mu
    var = (jnp.sum(c_lo * c_lo, axis=-1, keepdims=True) + jnp.sum(c_hi * c_hi, axis=-1, keepdims=True)) / D_MODEL
    inv = lax.rsqrt(var + NORM_EPS)
    o_lo = c_lo * inv * lng_ref[:, :HALF] + lnb_ref[:, :HALF]
    o_hi = c_hi * inv * lng_ref[:, HALF:] + lnb_ref[:, HALF:]
    o32_ref[:, :HALF] = o_lo
    o32_ref[:, HALF:] = o_hi
    o16_ref[:, :HALF] = o_lo.astype(BF16)
    o16_ref[:, HALF:] = o_hi.astype(BF16)


def moe_combine(ys, pos8, gate_t, h32, h16, s_gate, s_up, s_down, ln_g, ln_b, tb=128):
    m, d = h32.shape
    tb = min(tb, m)
    row = lambda w: pl.BlockSpec((tb, w), lambda i: (i, 0))
    return pl.pallas_call(
        functools.partial(_combine_kernel, tb=tb),
        grid=(m // tb,),
        in_specs=[pl.BlockSpec((TOP_K, tb), lambda i: (0, i), memory_space=pltpu.SMEM),
                  row(TOP_K), row(d), row(d),
                  _const_spec(s_gate.shape), _const_spec(s_up.shape), _const_spec(s_down.shape),
                  _const_spec((1, d)), _const_spec((1, d)),
                  pl.BlockSpec(memory_space=pl.ANY)],
        out_specs=[row(d), row(d)],
        out_shape=[jax.ShapeDtypeStruct((m, d), F32), jax.ShapeDtypeStruct((m, d), BF16)],
        scratch_shapes=[pltpu.VMEM((TOP_K, tb, d // 2), jnp.uint32), pltpu.SemaphoreType.DMA(())],
        compiler_params=_cparams("arbitrary"),
        name="moe_combine",
    )(pos8, gate_t, h32, h16, s_gate, s_up, s_down, ln_g.reshape(1, d), ln_b.reshape(1, d), ys)


def moe_layer(h32, h16, hpk, router, bias, w_gate, w_up, w_down, s_gate, s_up, s_down, ln_g, ln_b):
    m = h32.shape[0]
    eid8, rank8, gate8, counts = moe_route(h32, router, bias)
    cnt = counts[:, 0].astype(jnp.int32)
    seg = ((cnt + FFN_TM - 1) // FFN_TM) * FFN_TM
    ends = jnp.cumsum(seg)
    offs = ends - seg
    n_rows = m * TOP_K + N_EXPERTS * FFN_TM
    n_tiles = n_rows // FFN_TM
    experts = jnp.arange(N_EXPERTS, dtype=jnp.int32)
    pos8 = jnp.sum(jnp.where(eid8[:, :, None] == experts, offs, 0), axis=-1) + rank8
    tile_start = jnp.arange(n_tiles, dtype=jnp.int32) * FFN_TM
    tile_expert = jnp.minimum(jnp.sum((ends[None, :] <= tile_start[:, None]).astype(jnp.int32), axis=1),
                              N_EXPERTS - 1)
    n_used = (ends[-1:] // FFN_TM).astype(jnp.int32)
    xs = moe_dispatch(hpk, pos8, n_rows)
    ys = moe_experts(xs, tile_expert, n_used, w_gate.astype(BF16), w_up.astype(BF16), w_down.astype(BF16))
    return moe_combine(ys, pos8, gate8.T, h32, h16, s_gate.astype(BF16), s_up.astype(BF16),
                       s_down.astype(BF16), ln_g, ln_b)


def _even_mixer(h16, lb, in_proj, out_proj, hg_gn_w, rw_mix, rw_w0, rw_w2, rw_a0, rw_a2, rw_g2, rw_kk, rw_ka,
                rw_rk, rw_ln_w, rw_ln_b, batch, seq):
    hg_cols = 4 * HG_W
    main_end = hg_cols + 3 * RW_W
    c1 = main_end + RW_DECAY_RANK
    c2 = c1 + RW_A_RANK
    padc = lambda w, n: jnp.pad(w, ((0, 0), (0, n - w.shape[1])))
    w_low = jnp.concatenate([padc(in_proj[:, main_end:c1], 128), padc(in_proj[:, c1:c2], 128),
                             padc(in_proj[:, c2:], 256)], axis=1)
    p_hg = matmul(h16, in_proj[:, :hg_cols].astype(BF16), F32)
    p_main = matmul(h16, in_proj[:, hg_cols:main_end].astype(BF16), F32)
    p_low = matmul(h16, w_low.astype(BF16), F32)
    o_a = hgrn2(p_hg, lb, hg_gn_w, batch, seq)
    o_b = rwkv7(p_main, p_low, rw_mix, rw_w0, rw_w2, rw_a0, rw_a2, rw_g2, rw_kk, rw_ka, rw_rk, rw_ln_w,
                rw_ln_b, batch, seq)
    return jnp.concatenate([o_a, o_b.astype(BF16)], axis=-1)


def _odd_mixer(h16, in_proj, conv_w, conv_b, dt_bias, a_log, d_skip, norm_w, batch, seq):
    c1 = SSD_INNER
    c2 = c1 + SSD_CONV_DIM
    z = matmul(h16, in_proj[:, :c1].astype(BF16), BF16)
    xbc = matmul(h16, in_proj[:, c1:c2].astype(BF16), F32)
    dt = matmul(h16, jnp.pad(in_proj[:, c2:], ((0, 0), (0, 128 - SSD_HEADS))).astype(BF16), F32)
    return ssd(z, xbc, dt, conv_w, conv_b, dt_bias, a_log, d_skip, norm_w, batch, seq)


def kernel(x, hg_lb, ev_in_proj, ev_out_proj, hg_gn_w, rw_mix, rw_w0, rw_w2, rw_a0, rw_a2, rw_g2, rw_kk, rw_ka,
           rw_rk, rw_ln_w, rw_ln_b, od_in_proj, od_conv_w, od_conv_b, od_dt_bias, od_a_log, od_d, od_norm_w,
           od_out_proj, moe_router, moe_bias, moe_w_gate, moe_w_up, moe_w_down, sh_w_gate, sh_w_up, sh_w_down,
           ln1_g, ln1_b, ln2_g, ln2_b):
    batch, seq, d = x.shape
    lbs = jnp.cumsum(jax.nn.softmax(hg_lb.astype(F32), axis=0), axis=0)
    h32 = x.reshape(batch * seq, d)
    h16 = h32.astype(BF16)
    for l in range(DEPTH):
        if l % 2 == 0:
            e = l // 2
            mix = _even_mixer(h16, lbs[l], ev_in_proj[e], ev_out_proj[e], hg_gn_w[e], rw_mix[e], rw_w0[e],
                              rw_w2[e], rw_a0[e], rw_a2[e], rw_g2[e], rw_kk[e], rw_ka[e], rw_rk[e],
                              rw_ln_w[e], rw_ln_b[e], batch, seq)
            w_out = ev_out_proj[e]
        else:
            o = l // 2
            mix = _odd_mixer(h16, od_in_proj[o], od_conv_w[o], od_conv_b[o], od_dt_bias[o], od_a_log[o],
                             od_d[o], od_norm_w[o], batch, seq)
            w_out = od_out_proj[o]
        h32, h16, hpk = matmul_residual_ln(mix, w_out.astype(BF16), h32, ln1_g[l], ln1_b[l])
        h32, h16 = moe_layer(h32, h16, hpk, moe_router[l], moe_bias[l], moe_w_gate[l], moe_w_up[l],
                             moe_w_down[l], sh_w_gate[l], sh_w_up[l], sh_w_down[l], ln2_g[l], ln2_b[l])
    return h32.reshape(batch, seq, d)
```

```python
import functools
import math

import jax
import jax.numpy as jnp
import numpy as np
from jax import lax
from jax.experimental import pallas as pl
from jax.experimental.pallas import tpu as pltpu

F32 = jnp.float32
BF16 = jnp.bfloat16

D_MODEL = 2048
DEPTH = 2
CHUNK = 64

HG_HEADS = 8
HG_D = 128
HG_W = HG_HEADS * HG_D
RW_HEADS = 16
RW_N = 64
RW_W = RW_HEADS * RW_N
RW_DECAY_RANK = 64
RW_A_RANK = 64
RW_GATE_RANK = 160
RW_GN_EPS = 64e-5
RW_LOW_PAD = 512

SSD_INNER = 2 * D_MODEL
SSD_P = 64
SSD_HEADS = SSD_INNER // SSD_P
SSD_GROUPS = 8
SSD_HPG = SSD_HEADS // SSD_GROUPS
SSD_N = 128
SSD_CONV = 4
SSD_BC = SSD_GROUPS * SSD_N
SSD_CONV_DIM = SSD_INNER + 2 * SSD_BC
SSD_GW = SSD_HPG * SSD_P

N_EXPERTS = 64
TOP_K = 8
N_GROUPS = 8
TOPK_GROUPS = 4
EXPERT_DIM = 512
ROUTED_SCALE = 2.5

NORM_EPS = 1e-5
DN_ALPHA = (2 * DEPTH) ** 0.25

VMEM_LIMIT = 56 * 1024 * 1024


def _cparams(*sem):
    return pltpu.CompilerParams(dimension_semantics=sem, vmem_limit_bytes=VMEM_LIMIT)


def _const_spec(shape):
    nd = len(shape)
    return pl.BlockSpec(shape, lambda *_: (0,) * nd, pipeline_mode=pl.Buffered(1))


def _split3(x):
    hi = x.astype(BF16)
    r1 = x - hi.astype(F32)
    mid = r1.astype(BF16)
    lo = (r1 - mid.astype(F32)).astype(BF16)
    return hi, mid, lo


def _dot(a, b):
    return jnp.dot(a, b, preferred_element_type=F32)


def _dot_exact_lhs(m_bf16, x):
    hi, mid, lo = _split3(x)
    return _dot(m_bf16, hi) + _dot(m_bf16, mid) + _dot(m_bf16, lo)


def _dot_exact_rhs(x, m_bf16):
    hi, mid, lo = _split3(x)
    return _dot(hi, m_bf16) + _dot(mid, m_bf16) + _dot(lo, m_bf16)


def _sigmoid(x):
    return 1.0 / (1.0 + jnp.exp(-x))


def _silu(x):
    return x * _sigmoid(x)


def _softplus(x):
    return jnp.maximum(x, 0.0) + jnp.log(1.0 + jnp.exp(-jnp.abs(x)))


def _mm_kernel(x_ref, w_ref, o_ref):
    o_ref[...] = _dot(x_ref[...], w_ref[...]).astype(o_ref.dtype)


def matmul(x, w, out_dtype, tm=1024, tn=512):
    m, k = x.shape
    n = w.shape[1]
    tm = min(tm, m)
    tn = min(tn, n)
    assert m % tm == 0 and n % tn == 0
    return pl.pallas_call(
        _mm_kernel,
        grid=(m // tm, n // tn),
        in_specs=[pl.BlockSpec((tm, k), lambda i, j: (i, 0)),
                  pl.BlockSpec((k, tn), lambda i, j: (0, j))],
        out_specs=pl.BlockSpec((tm, tn), lambda i, j: (i, j)),
        out_shape=jax.ShapeDtypeStruct((m, n), out_dtype),
        compiler_params=_cparams("parallel", "arbitrary"),
        name="matmul",
    )(x, w)


def _layer_norm_rows(y, g, b):
    mu = jnp.mean(y, axis=-1, keepdims=True)
    yc = y - mu
    var = jnp.mean(yc * yc, axis=-1, keepdims=True)
    return yc * lax.rsqrt(var + NORM_EPS) * g + b


def _bf16_bits(x):
    u = lax.bitcast_convert_type(x, jnp.uint32)
    u = u + jnp.uint32(0x7FFF) + ((u >> 16) & jnp.uint32(1))
    return u & jnp.uint32(0xFFFF0000)


def _pack_halves(lo, hi):
    return (_bf16_bits(lo) >> 16) | _bf16_bits(hi)


def _unpack_halves(u):
    lo = lax.bitcast_convert_type(u << 16, F32)
    hi = lax.bitcast_convert_type(u & jnp.uint32(0xFFFF0000), F32)
    return lo, hi


HALF = D_MODEL // 2
ROW_SUB = 8
ROW_LANES = HALF // ROW_SUB


def _store_row_tiles(ref, packed):
    n = packed.shape[0]
    for c in range(ROW_SUB):
        ref[pl.ds(c, n, stride=ROW_SUB), :] = packed[:, c * ROW_LANES:(c + 1) * ROW_LANES]


def _load_row_tiles(ref, n):
    return jnp.concatenate([ref[pl.ds(c, n, stride=ROW_SUB), :] for c in range(ROW_SUB)], axis=1)


def _ln_outputs(y, g_ref, b_ref, o32_ref, o16_ref, opk_ref):
    o = _layer_norm_rows(y, g_ref[...], b_ref[...])
    o32_ref[...] = o
    o16_ref[...] = o.astype(BF16)
    _store_row_tiles(opk_ref, _pack_halves(o[:, :HALF], o[:, HALF:]))


def _mm_ln_kernel(x_ref, w_ref, h_ref, g_ref, b_ref, o32_ref, o16_ref, opk_ref):
    y = _dot(x_ref[...], w_ref[...]) + DN_ALPHA * h_ref[...]
    _ln_outputs(y, g_ref, b_ref, o32_ref, o16_ref, opk_ref)


def _mm2_ln_kernel(xa_ref, xb_ref, gate_ref, w_ref, h_ref, g_ref, b_ref, o32_ref, o16_ref, opk_ref):
    ka = xa_ref.shape[1]
    xb = (xb_ref[...] * gate_ref[...]).astype(BF16)
    y = _dot(xa_ref[...], w_ref[:ka, :]) + _dot(xb, w_ref[ka:, :]) + DN_ALPHA * h_ref[...]
    _ln_outputs(y, g_ref, b_ref, o32_ref, o16_ref, opk_ref)


def matmul_residual_ln(xs, w, h, g, b, tm=256):
    m = xs[0].shape[0]
    k, d = w.shape
    tm = min(tm, m)
    assert m % tm == 0
    row = lambda a: pl.BlockSpec((tm, a.shape[1]), lambda i: (i, 0))
    return pl.pallas_call(
        _mm_ln_kernel if len(xs) == 1 else _mm2_ln_kernel,
        grid=(m // tm,),
        in_specs=[row(a) for a in xs] + [_const_spec((k, d)), row(h), _const_spec((1, d)), _const_spec((1, d))],
        out_specs=[pl.BlockSpec((tm, d), lambda i: (i, 0)),
                   pl.BlockSpec((tm, d), lambda i: (i, 0)),
                   pl.BlockSpec((tm * ROW_SUB, ROW_LANES), lambda i: (i, 0))],
        out_shape=[jax.ShapeDtypeStruct((m, d), F32), jax.ShapeDtypeStruct((m, d), BF16),
                   jax.ShapeDtypeStruct((m * ROW_SUB, ROW_LANES), jnp.uint32)],
        compiler_params=_cparams("parallel"),
        name="matmul_residual_ln",
    )(*xs, w, h, g.reshape(1, d), b.reshape(1, d))


HG_LEVELS = (32, 16, 8, 4, 2, 1)


def _hgrn2_constants():
    t = np.arange(CHUNK)[:, None]
    u = np.arange(CHUNK)[None, :]
    blocks = [(u <= t)]
    for b in HG_LEVELS:
        start = (t // b) * b
        end = start + b - 1
        blocks.append((u >= start) & (u <= t))
        blocks.append((u > t) & (u <= end))
    blocks.append(u > t)
    cm = np.concatenate(blocks, axis=0).astype(np.float32)
    s = u
    masks = [((t // (2 * b)) == (s // (2 * b))) & ((t // b) % 2 == 1) & ((s // b) % 2 == 0) for b in HG_LEVELS]
    return cm, np.stack(masks).astype(np.float32)


HG_HPS = 4


def _hgrn2_kernel(q_ref, f_ref, i_ref, g_ref, lb_ref, gnw_ref, cm_ref, mask_ref, o_ref, st_ref, *, n_chunks):
    @pl.when(pl.program_id(2) == 0)
    def _():
        st_ref[...] = jnp.zeros_like(st_ref)

    lb = lb_ref[...]
    cm = cm_ref[...]
    nt = (((1,), (1,)), ((), ()))
    tail = CHUNK + 2 * CHUNK * len(HG_LEVELS)

    def chunk(c, carry):
        rows = pl.ds(pl.multiple_of(c * CHUNK, CHUNK), CHUNK)
        fp_all = f_ref[rows, :]
        lf_all = jnp.log(lb + (1.0 - lb) * _sigmoid(fp_all))
        k_all = (1.0 - lb) * _sigmoid(-fp_all)
        seg_all = _dot_exact_lhs(cm, lf_all)
        for hh in range(HG_HPS):
            cols = slice(hh * HG_D, (hh + 1) * HG_D)
            q = q_ref[rows, cols]
            v = i_ref[rows, cols]
            g = g_ref[rows, cols]
            k = k_all[:, cols]
            seg = seg_all[:, cols]
            bc = seg[0:CHUNK]
            st = st_ref[hh]
            o = lax.dot_general((q * jnp.exp(bc)).astype(BF16), st.astype(BF16), nt, preferred_element_type=F32)
            att = jnp.zeros((CHUNK, CHUNK), F32)
            for li in range(len(HG_LEVELS)):
                base = CHUNK + 2 * CHUNK * li
                qs = q * jnp.exp(seg[base:base + CHUNK])
                ks = k * jnp.exp(seg[base + CHUNK:base + 2 * CHUNK])
                a_l = lax.dot_general(qs.astype(BF16), ks.astype(BF16), nt, preferred_element_type=F32)
                att = att + a_l * mask_ref[li]
            o = o + _dot(att.astype(BF16), v.astype(BF16)) + jnp.sum(q * k, axis=-1, keepdims=True) * v
            kd = k * jnp.exp(seg[tail:tail + CHUNK])
            st_ref[hh] = st * jnp.exp(bc[CHUNK - 1:CHUNK]) + _dot(v.T.astype(BF16), kd.astype(BF16))
            ms = jnp.mean(o * o, axis=-1, keepdims=True)
            o_ref[rows, cols] = (o * lax.rsqrt(ms + NORM_EPS) * gnw_ref[:, cols] * _silu(g)).astype(o_ref.dtype)
        return carry

    lax.fori_loop(0, n_chunks, chunk, 0)


def hgrn2(p_hg, lb, gn_w, batch, seq, tb=256):
    tb = min(tb, seq)
    nt = seq // tb
    hg = HG_HEADS // HG_HPS
    w = HG_HPS * HG_D
    cm, masks = _hgrn2_constants()
    spec = lambda off: pl.BlockSpec((tb, w), lambda b, h, t: (b * nt + t, off * hg + h))
    return pl.pallas_call(
        functools.partial(_hgrn2_kernel, n_chunks=tb // CHUNK),
        grid=(batch, hg, nt),
        in_specs=[spec(0), spec(1), spec(2), spec(3),
                  pl.BlockSpec((1, w), lambda b, h, t: (0, h)),
                  pl.BlockSpec((1, w), lambda b, h, t: (0, h)),
                  _const_spec(cm.shape), _const_spec(masks.shape)],
        out_specs=pl.BlockSpec((tb, w), lambda b, h, t: (b * nt + t, h)),
        out_shape=jax.ShapeDtypeStruct((batch * seq, HG_W), BF16),
        scratch_shapes=[pltpu.VMEM((HG_HPS, HG_D, HG_D), F32)],
        compiler_params=_cparams("parallel", "parallel", "arbitrary"),
        name="hgrn2",
    )(p_hg, p_hg, p_hg, p_hg, lb.reshape(1, HG_W), gn_w.reshape(1, HG_W),
      jnp.asarray(cm, BF16), jnp.asarray(masks))


def _rwkv_prep_kernel(pm_ref, pmprev_ref, plo_ref, ploprev_ref, mixm_ref, mixl_ref, w0_ref, w2_ref, a0_ref,
                      a2_ref, g2_ref, r_o, k_o, v_o, w_o, a_o, g_o, *, tb, seq):
    first = (pl.program_id(0) * tb) % seq == 0
    keep = jnp.where(first, 0.0, 1.0)

    def shift_mix(cur, prev_blk, mix):
        row = lax.broadcasted_iota(jnp.int32, cur.shape, 0)
        shifted = jnp.where(row == 0, prev_blk[7:8, :] * keep, pltpu.roll(cur, 1, 0))
        return cur + (shifted - cur) * mix

    for idx, out in enumerate((r_o, k_o, v_o)):
        cols = slice(idx * RW_W, (idx + 1) * RW_W)
        out[...] = shift_mix(pm_ref[:, cols], pmprev_ref[:, cols], mixm_ref[:, cols])
    low = shift_mix(plo_ref[...], ploprev_ref[...], mixl_ref[...])
    wl = jnp.tanh(low[:, 0:128]).astype(BF16)
    al = low[:, 128:256].astype(BF16)
    gl = _sigmoid(low[:, 256:512]).astype(BF16)
    w_log = -_softplus(-(w0_ref[...] + _dot(wl, w2_ref[...]))) - 0.5
    w_o[...] = jnp.exp(-jnp.exp(w_log))
    a_o[...] = _sigmoid(a0_ref[...] + _dot(al, a2_ref[...]))
    g_o[...] = _dot(gl, g2_ref[...])


def rwkv_prep(p_main, p_low, mix_main, mix_low, w0, w2p, a0, a2p, g2p, seq, tb=256):
    m = p_main.shape[0]
    tb = min(tb, seq)
    prev = lambda i: (jnp.maximum(i * (tb // 8) - 1, 0), 0)
    out = jax.ShapeDtypeStruct((m, RW_W), F32)
    ospec = pl.BlockSpec((tb, RW_W), lambda i: (i, 0))
    return pl.pallas_call(
        functools.partial(_rwkv_prep_kernel, tb=tb, seq=seq),
        grid=(m // tb,),
        in_specs=[pl.BlockSpec((tb, 3 * RW_W), lambda i: (i, 0)),
                  pl.BlockSpec((8, 3 * RW_W), prev),
                  pl.BlockSpec((tb, RW_LOW_PAD), lambda i: (i, 0)),
                  pl.BlockSpec((8, RW_LOW_PAD), prev),
                  _const_spec((1, 3 * RW_W)), _const_spec((1, RW_LOW_PAD)),
                  _const_spec((1, RW_W)), _const_spec((128, RW_W)),
                  _const_spec((1, RW_W)), _const_spec((128, RW_W)),
                  _const_spec((256, RW_W))],
        out_specs=[ospec] * 6,
        out_shape=[out] * 6,
        compiler_params=_cparams("parallel"),
        name="rwkv_prep",
    )(p_main, p_main, p_low, p_low, mix_main, mix_low, w0, w2p, a0, a2p, g2p)


def _rwkv_scan_kernel(r_ref, k_ref, v_ref, w_ref, a_ref, kkp_ref, kap_ref, rkp_ref, lnw_ref, lnb_ref,
                      o_ref, s_ref, vec_ref, *, tt):
    @pl.when(pl.program_id(0) == 0)
    def _():
        s_ref[...] = jnp.zeros_like(s_ref)

    kkp = kkp_ref[...]
    kap = kap_ref[...]
    rkp = rkp_ref[...]
    lnw = lnw_ref[...]
    lnb = lnb_ref[...]

    def step(t, carry):
        kt = k_ref[t]
        at = a_ref[t]
        rt = r_ref[t]
        vt = v_ref[t]
        kk = kt * kkp
        nrm = jnp.sqrt(jnp.sum(kk * kk, axis=0, keepdims=True))
        kk = kk / jnp.maximum(nrm, 1e-12)
        kh = kt * (1.0 + (at - 1.0) * kap)
        vec_ref[0] = -kk
        vec_ref[1] = kk * at
        vec_ref[2] = kh
        sa = jnp.zeros((RW_N, kt.shape[1]), F32)
        for j in range(RW_N):
            sa = sa + s_ref[j] * vec_ref[0, j:j + 1, :]
        y = jnp.zeros_like(sa)
        for j in range(RW_N):
            sj = (s_ref[j] * w_ref[t, j:j + 1, :] + sa * vec_ref[1, j:j + 1, :]
                  + vt * vec_ref[2, j:j + 1, :])
            s_ref[j] = sj
            y = y + sj * r_ref[t, j:j + 1, :]
        mu = jnp.mean(y, axis=0, keepdims=True)
        yc = y - mu
        var = jnp.mean(yc * yc, axis=0, keepdims=True)
        yn = yc * lax.rsqrt(var + RW_GN_EPS) * lnw + lnb
        bonus = jnp.sum(rt * kh * rkp, axis=0, keepdims=True) * vt
        o_ref[t] = yn + bonus
        return carry

    lax.fori_loop(0, tt, step, 0)


def rwkv_scan(r, k, v, w, a, kkp, kap, rkp, lnw, lnb, tt=16):
    t, n, lanes = r.shape
    tt = min(tt, t)
    blk = pl.BlockSpec((tt, n, lanes), lambda i: (i, 0, 0))
    par = _const_spec((n, lanes))
    return pl.pallas_call(
        functools.partial(_rwkv_scan_kernel, tt=tt),
        grid=(t // tt,),
        in_specs=[blk] * 5 + [par] * 5,
        out_specs=blk,
        out_shape=jax.ShapeDtypeStruct((t, n, lanes), F32),
        scratch_shapes=[pltpu.VMEM((n, n, lanes), F32), pltpu.VMEM((3, n, lanes), F32)],
        compiler_params=_cparams("arbitrary"),
        name="rwkv_scan",
    )(r, k, v, w, a, kkp, kap, rkp, lnw, lnb)


def _time_major(x, batch, seq):
    return x.reshape(batch, seq, RW_HEADS, RW_N).transpose(1, 3, 0, 2).reshape(seq, RW_N, batch * RW_HEADS)


def _token_major(x, batch, seq):
    return x.reshape(seq, RW_N, batch, RW_HEADS).transpose(2, 0, 3, 1).reshape(batch * seq, RW_W)


def _head_param(p, batch):
    return jnp.tile(p.reshape(RW_HEADS, RW_N).T, (1, batch))


def _pad_rows(w, rows):
    return jnp.pad(w, ((0, rows - w.shape[0]), (0, 0)))


def rwkv7(p_main, p_low, mix, w0, w2, a0, a2, g2, k_k, k_a, r_k, ln_w, ln_b, batch, seq):
    mix_main = mix[:3 * RW_W].reshape(1, -1)
    lo = mix[3 * RW_W:]
    c1 = RW_DECAY_RANK
    c2 = c1 + RW_A_RANK
    mix_low = jnp.concatenate([jnp.pad(lo[:c1], (0, 128 - RW_DECAY_RANK)),
                               jnp.pad(lo[c1:c2], (0, 128 - RW_A_RANK)),
                               jnp.pad(lo[c2:], (0, 256 - RW_GATE_RANK))]).reshape(1, RW_LOW_PAD)
    r, k, v, w, a, g = rwkv_prep(
        p_main, p_low, mix_main, mix_low, w0.reshape(1, -1), _pad_rows(w2, 128).astype(BF16),
        a0.reshape(1, -1), _pad_rows(a2, 128).astype(BF16), _pad_rows(g2, 256).astype(BF16), seq)
    tm = lambda x: _time_major(x, batch, seq)
    hp = lambda p: _head_param(p.reshape(-1), batch)
    o = rwkv_scan(tm(r), tm(k), tm(v), tm(w), tm(a), hp(k_k), hp(k_a), hp(r_k), hp(ln_w), hp(ln_b))
    return _token_major(o, batch, seq), g


def _ssd_kernel(z_ref, xbc_ref, dt_ref, cw_ref, cb_ref, dtb_ref, alog_ref, dexp_ref, nw_ref, ex_ref, tri_ref,
                o_ref, xpad_ref, xact_ref, st_ref, y_ref):
    c = pl.program_id(1)

    @pl.when(c == 0)
    def _():
        st_ref[...] = jnp.zeros_like(st_ref)
        xpad_ref[0:8, :] = jnp.zeros((8, SSD_CONV_DIM), F32)

    xpad_ref[8:8 + CHUNK, :] = xbc_ref[...]
    ct = 512
    for j in range(SSD_CONV_DIM // ct):
        cols = slice(j * ct, (j + 1) * ct)
        win = xpad_ref[:, cols]
        acc = cw_ref[0:1, cols] * win
        for tap in range(1, SSD_CONV):
            acc = pltpu.roll(acc, 1, 0) + cw_ref[tap:tap + 1, cols] * win
        xact_ref[:, cols] = _silu(acc[8:] + cb_ref[:, cols])
    xpad_ref[0:8, :] = xpad_ref[CHUNK:CHUNK + 8, :]

    ex = ex_ref[...]
    dtv = _softplus(dt_ref[...] + dtb_ref[...])
    da = dtv * (-jnp.exp(alog_ref[...]))
    acum = _dot_exact_lhs(tri_ref[...], da)
    acum_t = acum.T
    dtx = _dot_exact_rhs(dtv, ex)
    acx = _dot_exact_rhs(acum, ex)
    alx = acx[CHUNK - 1:CHUNK, :]
    e_in = jnp.exp(acx)
    e_out = jnp.exp(alx - acx)
    e_last = jnp.exp(alx)
    xs = xact_ref[:, 0:SSD_INNER]
    xdt = xs * dtx
    ti = lax.broadcasted_iota(jnp.int32, (CHUNK, CHUNK), 0)
    si = lax.broadcasted_iota(jnp.int32, (CHUNK, CHUNK), 1)
    causal = ti >= si
    lane = lax.broadcasted_iota(jnp.int32, (CHUNK, 128), 1)
    nt = (((1,), (1,)), ((), ()))

    for g in range(SSD_GROUPS):
        bg = xact_ref[:, SSD_INNER + g * SSD_N:SSD_INNER + (g + 1) * SSD_N]
        cg = xact_ref[:, SSD_INNER + SSD_BC + g * SSD_N:SSD_INNER + SSD_BC + (g + 1) * SSD_N]
        cgb = cg.astype(BF16)
        cb = lax.dot_general(cgb, bg.astype(BF16), nt, preferred_element_type=F32)
        gc = slice(g * SSD_GW, (g + 1) * SSD_GW)
        sgt = st_ref[g]
        y_g = _dot(cgb, sgt.astype(BF16)) * e_in[:, gc]
        xdt_g = xdt[:, gc]
        for m in range(SSD_HPG // 2):
            xp = xdt_g[:, m * 128:(m + 1) * 128]
            yp = y_g[:, m * 128:(m + 1) * 128]
            for half in range(2):
                h = g * SSD_HPG + 2 * m + half
                seg = acum[:, h:h + 1] - acum_t[h:h + 1, :]
                lmat = jnp.where(causal, jnp.exp(jnp.minimum(seg, 0.0)), 0.0)
                rhs = jnp.where((lane >= 64) == (half == 1), xp, 0.0)
                yp = yp + _dot((cb * lmat).astype(BF16), rhs.astype(BF16))
            y_ref[:, g * SSD_GW + m * 128:g * SSD_GW + (m + 1) * 128] = yp
        st_ref[g] = sgt * e_last[:, gc] + _dot(bg.T.astype(BF16), (xdt_g * e_out[:, gc]).astype(BF16))

    for g in range(SSD_GROUPS):
        gc = slice(g * SSD_GW, (g + 1) * SSD_GW)
        y = y_ref[:, gc] + dexp_ref[:, gc] * xact_ref[:, gc]
        y = y * _silu(z_ref[:, gc].astype(F32))
        ms = jnp.mean(y * y, axis=-1, keepdims=True)
        o_ref[:, gc] = (y * lax.rsqrt(ms + NORM_EPS) * nw_ref[:, gc]).astype(o_ref.dtype)


def ssd(z, xbc, dt, conv_w, conv_b, dt_bias, a_log, d_skip, norm_w, batch, seq):
    nc = seq // CHUNK
    pad = lambda v: jnp.pad(v.reshape(1, -1), ((0, 0), (0, 128 - SSD_HEADS)))
    ex = np.zeros((128, SSD_INNER), np.float32)
    for h in range(SSD_HEADS):
        ex[h, h * SSD_P:(h + 1) * SSD_P] = 1.0
    tri = np.tril(np.ones((CHUNK, CHUNK), np.float32))
    row = lambda w: pl.BlockSpec((CHUNK, w), lambda b, c: (b * nc + c, 0))
    return pl.pallas_call(
        _ssd_kernel,
        grid=(batch, nc),
        in_specs=[row(SSD_INNER), row(SSD_CONV_DIM), row(128),
                  _const_spec((SSD_CONV, SSD_CONV_DIM)), _const_spec((1, SSD_CONV_DIM)),
                  _const_spec((1, 128)), _const_spec((1, 128)),
                  _const_spec((1, SSD_INNER)), _const_spec((1, SSD_INNER)),
                  _const_spec((128, SSD_INNER)), _const_spec((CHUNK, CHUNK))],
        out_specs=row(SSD_INNER),
        out_shape=jax.ShapeDtypeStruct((batch * seq, SSD_INNER), BF16),
        scratch_shapes=[pltpu.VMEM((CHUNK + 8, SSD_CONV_DIM), F32),
                        pltpu.VMEM((CHUNK, SSD_CONV_DIM), F32),
                        pltpu.VMEM((SSD_GROUPS, SSD_N, SSD_GW), F32),
                        pltpu.VMEM((CHUNK, SSD_INNER), F32)],
        compiler_params=_cparams("parallel", "arbitrary"),
        name="ssd",
    )(z, xbc, dt, conv_w, conv_b.reshape(1, -1), pad(dt_bias), pad(a_log),
      jnp.repeat(d_skip, SSD_P).reshape(1, -1), norm_w.reshape(1, -1), jnp.asarray(ex, BF16),
      jnp.asarray(tri, BF16))


FFN_TM = 256


def _router_kernel(x_ref, rt_ref, bias_ref, up_ref, ls_ref, eid_o, rank_o, gate_o, cnt_o, carry_ref, *, tb):
    @pl.when(pl.program_id(0) == 0)
    def _():
        carry_ref[...] = jnp.zeros_like(carry_ref)

    nt = (((1,), (1,)), ((), ()))
    x = x_ref[...]
    rt = rt_ref[...]
    xh = x.astype(BF16)
    xl = (x - xh.astype(F32)).astype(BF16)
    rh = rt.astype(BF16)
    rl = (rt - rh.astype(F32)).astype(BF16)
    dg = lambda a, b: lax.dot_general(a, b, nt, preferred_element_type=F32)
    logits = dg(rh, xh) + dg(rh, xl) + dg(rl, xh)
    scores = _sigmoid(logits)
    reps = tb // 128
    wide = lambda a: jnp.concatenate([a] * reps, axis=1) if reps > 1 else a
    biased = scores + wide(bias_ref[...])
    neg = -jnp.inf

    io8 = lax.broadcasted_iota(jnp.int32, (8, tb), 0)
    blocks, gs = [], []
    for g in range(N_GROUPS):
        blk = biased[8 * g:8 * g + 8, :]
        m1 = jnp.max(blk, axis=0, keepdims=True)
        first = jnp.min(jnp.where(blk == m1, io8, 8), axis=0, keepdims=True)
        m2 = jnp.max(jnp.where(io8 == first, neg, blk), axis=0, keepdims=True)
        blocks.append(blk)
        gs.append(m1 + m2)
    masked = []
    for g in range(N_GROUPS):
        ahead = jnp.zeros((1, tb), jnp.int32)
        for o in range(N_GROUPS):
            if o == g:
                continue
            beats = (gs[o] > gs[g]) | ((gs[o] == gs[g]) & (o < g))
            ahead = ahead + jnp.where(beats, 1, 0)
        masked.append(jnp.where(ahead < TOPK_GROUPS, blocks[g], neg))
    masked = jnp.concatenate(masked, axis=0)
    eidx = lax.broadcasted_iota(jnp.int32, (N_EXPERTS, tb), 0)
    ahead = jnp.zeros((N_EXPERTS, tb), jnp.int32)
    for o in range(N_EXPERTS):
        row = masked[o:o + 1, :]
        beats = (row > masked) | ((row == masked) & (eidx > o))
        ahead = ahead + jnp.where(beats, 1, 0)
    sel = ahead < TOP_K
    self_ = jnp.where(sel, 1.0, 0.0)
    wts = jnp.where(sel, scores, 0.0)
    gate = wts / jnp.sum(wts, axis=0, keepdims=True) * ROUTED_SCALE

    selb = self_.astype(BF16)
    carry = carry_ref[...]
    rank = _dot(selb, up_ref[...]) + wide(carry)
    new_carry = carry + _dot(selb, jnp.ones((tb, 128), BF16))
    carry_ref[...] = new_carry
    cnt_o[...] = new_carry
    before = _dot(ls_ref[...], selb)
    eidf = eidx.astype(F32)
    for j in range(TOP_K):
        hit = sel & (before == float(j))
        pick = lambda a: jnp.sum(jnp.where(hit, a, 0.0), axis=0, keepdims=True)
        eid_o[j:j + 1, :] = pick(eidf).astype(jnp.int32)
        rank_o[j:j + 1, :] = pick(rank).astype(jnp.int32)
        gate_o[j:j + 1, :] = pick(gate)


def moe_route(h32, router, bias, tb=256):
    m = h32.shape[0]
    tb = min(tb, m)
    up = np.triu(np.ones((tb, tb), np.float32), k=1)
    ls = np.tril(np.ones((N_EXPERTS, N_EXPERTS), np.float32), k=-1)
    slot = lambda dt: jax.ShapeDtypeStruct((TOP_K, m), dt)
    return pl.pallas_call(
        functools.partial(_router_kernel, tb=tb),
        grid=(m // tb,),
        in_specs=[pl.BlockSpec((tb, D_MODEL), lambda i: (i, 0)),
                  _const_spec((N_EXPERTS, D_MODEL)), _const_spec((N_EXPERTS, 128)),
                  _const_spec((tb, tb)), _const_spec((N_EXPERTS, N_EXPERTS))],
        out_specs=[pl.BlockSpec((TOP_K, tb), lambda i: (0, i))] * 3
                  + [pl.BlockSpec((N_EXPERTS, 128), lambda i: (0, 0))],
        out_shape=[slot(jnp.int32), slot(jnp.int32), slot(F32),
                   jax.ShapeDtypeStruct((N_EXPERTS, 128), F32)],
        scratch_shapes=[pltpu.VMEM((N_EXPERTS, 128), F32)],
        compiler_params=_cparams("arbitrary"),
        name="moe_route",
    )(h32, router.T, jnp.broadcast_to(bias.reshape(N_EXPERTS, 1), (N_EXPERTS, 128)),
      jnp.asarray(up, BF16), jnp.asarray(ls, BF16))


def _frame_rows(p):
    return pl.ds(pl.multiple_of(p * ROW_SUB, ROW_SUB), ROW_SUB)


def _dispatch_kernel(pos_ref, x_ref, xs_in, xs_out, sem, *, tb):
    del xs_in

    def copy(t, p):
        return pltpu.make_async_copy(x_ref.at[_frame_rows(t), :], xs_out.at[_frame_rows(p), :], sem)

    def issue(t, c):
        for j in range(TOP_K):
            copy(t, pos_ref[j, t]).start()
        return c

    lax.fori_loop(0, tb, issue, 0)

    def drain(t, c):
        for j in range(TOP_K):
            copy(0, 0).wait()
        return c

    lax.fori_loop(0, tb, drain, 0)


def moe_dispatch(xpk, pos8, n_rows, tb=256):
    m = xpk.shape[0] // ROW_SUB
    tb = min(tb, m)
    zeros = jnp.zeros((n_rows * ROW_SUB, ROW_LANES), xpk.dtype)
    return pl.pallas_call(
        functools.partial(_dispatch_kernel, tb=tb),
        grid=(m // tb,),
        in_specs=[pl.BlockSpec((TOP_K, tb), lambda i: (0, i), memory_space=pltpu.SMEM),
                  pl.BlockSpec((tb * ROW_SUB, ROW_LANES), lambda i: (i, 0)),
                  pl.BlockSpec(memory_space=pl.ANY)],
        out_specs=pl.BlockSpec(memory_space=pl.ANY),
        out_shape=jax.ShapeDtypeStruct(zeros.shape, xpk.dtype),
        scratch_shapes=[pltpu.SemaphoreType.DMA(())],
        input_output_aliases={2: 0},
        compiler_params=_cparams("arbitrary"),
        name="moe_dispatch",
    )(pos8, xpk, zeros)


def _ffn_kernel(te_ref, nu_ref, xs_ref, wg_ref, wu_ref, wd_ref, ys_ref):
    del te_ref

    @pl.when(pl.program_id(0) < nu_ref[0])
    def _():
        lo, hi = _unpack_halves(_load_row_tiles(xs_ref, FFN_TM))
        lo = lo.astype(BF16)
        hi = hi.astype(BF16)
        hg = _dot(lo, wg_ref[0, :HALF, :]) + _dot(hi, wg_ref[0, HALF:, :])
        hu = _dot(lo, wu_ref[0, :HALF, :]) + _dot(hi, wu_ref[0, HALF:, :])
        y = _dot((_silu(hg) * hu).astype(BF16), wd_ref[0])
        _store_row_tiles(ys_ref, _pack_halves(y[:, :HALF], y[:, HALF:]))


def moe_experts(xs, tile_expert, n_used, w_gate, w_up, w_down):
    n_tiles = xs.shape[0] // (FFN_TM * ROW_SUB)
    wspec = lambda shp: pl.BlockSpec((1,) + shp, lambda i, te, nu: (te[i], 0, 0))
    rows = pl.BlockSpec((FFN_TM * ROW_SUB, ROW_LANES), lambda i, te, nu: (i, 0))
    grid_spec = pltpu.PrefetchScalarGridSpec(
        num_scalar_prefetch=2,
        grid=(n_tiles,),
        in_specs=[rows, wspec((D_MODEL, EXPERT_DIM)), wspec((D_MODEL, EXPERT_DIM)),
                  wspec((EXPERT_DIM, D_MODEL))],
        out_specs=rows,
    )
    return pl.pallas_call(
        _ffn_kernel,
        grid_spec=grid_spec,
        out_shape=jax.ShapeDtypeStruct(xs.shape, xs.dtype),
        compiler_params=_cparams("arbitrary"),
        name="moe_experts",
    )(tile_expert, n_used, xs, w_gate, w_up, w_down)


def _combine_kernel(pos_ref, gate_ref, h32_ref, h16_ref, sg_ref, su_ref, sd_ref, lng_ref, lnb_ref, ys_hbm,
                    o32_ref, o16_ref, buf_ref, sem, *, tb):
    def copy(p, j, t):
        return pltpu.make_async_copy(ys_hbm.at[_frame_rows(p), :], buf_ref.at[j, _frame_rows(t), :], sem)

    def issue(t, c):
        for j in range(TOP_K):
            copy(pos_ref[j, t], j, t).start()
        return c

    lax.fori_loop(0, tb, issue, 0)

    x = h16_ref[...]
    hdn = (_silu(_dot(x, sg_ref[...])) * _dot(x, su_ref[...])).astype(BF16)
    shared = _dot(hdn, sd_ref[...])
    acc_lo = shared[:, :HALF] + DN_ALPHA * h32_ref[:, :HALF]
    acc_hi = shared[:, HALF:] + DN_ALPHA * h32_ref[:, HALF:]

    def drain(t, c):
        for j in range(TOP_K):
            copy(0, 0, 0).wait()
        return c

    lax.fori_loop(0, tb, drain, 0)

    for j in range(TOP_K):
        lo, hi = _unpack_halves(_load_row_tiles(buf_ref.at[j], tb))
        gj = gate_ref[:, j:j + 1]
        acc_lo = acc_lo + gj * lo
        acc_hi = acc_hi + gj * hi
    mu = (jnp.sum(acc_lo, axis=-1, keepdims=True) + jnp.sum(acc_hi, axis=-1, keepdims=True)) / D_MODEL
    c_lo = acc_lo - mu
    c_hi = acc_hi - mu
    var = (jnp.sum(c_lo * c_lo, axis=-1, keepdims=True) + jnp.sum(c_hi * c_hi, axis=-1, keepdims=True)) / D_MODEL
    inv = lax.rsqrt(var + NORM_EPS)
    o_lo = c_lo * inv * lng_ref[:, :HALF] + lnb_ref[:, :HALF]
    o_hi = c_hi * inv * lng_ref[:, HALF:] + lnb_ref[:, HALF:]
    o32_ref[:, :HALF] = o_lo
    o32_ref[:, HALF:] = o_hi
    o16_ref[:, :HALF] = o_lo.astype(BF16)
    o16_ref[:, HALF:] = o_hi.astype(BF16)


def moe_combine(ys, pos8, gate_t, h32, h16, s_gate, s_up, s_down, ln_g, ln_b, tb=128):
    m, d = h32.shape
    tb = min(tb, m)
    row = lambda w: pl.BlockSpec((tb, w), lambda i: (i, 0))
    return pl.pallas_call(
        functools.partial(_combine_kernel, tb=tb),
        grid=(m // tb,),
        in_specs=[pl.BlockSpec((TOP_K, tb), lambda i: (0, i), memory_space=pltpu.SMEM),
                  row(TOP_K), row(d), row(d),
                  _const_spec(s_gate.shape), _const_spec(s_up.shape), _const_spec(s_down.shape),
                  _const_spec((1, d)), _const_spec((1, d)),
                  pl.BlockSpec(memory_space=pl.ANY)],
        out_specs=[row(d), row(d)],
        out_shape=[jax.ShapeDtypeStruct((m, d), F32), jax.ShapeDtypeStruct((m, d), BF16)],
        scratch_shapes=[pltpu.VMEM((TOP_K, tb * ROW_SUB, ROW_LANES), jnp.uint32), pltpu.SemaphoreType.DMA(())],
        compiler_params=_cparams("arbitrary"),
        name="moe_combine",
    )(pos8, gate_t, h32, h16, s_gate, s_up, s_down, ln_g.reshape(1, d), ln_b.reshape(1, d), ys)


def moe_layer(h32, h16, hpk, router, bias, w_gate, w_up, w_down, s_gate, s_up, s_down, ln_g, ln_b):
    m = h32.shape[0]
    eid8, rank8, gate8, counts = moe_route(h32, router, bias)
    cnt = counts[:, 0].astype(jnp.int32)
    seg = ((cnt + FFN_TM - 1) // FFN_TM) * FFN_TM
    ends = jnp.cumsum(seg)
    offs = ends - seg
    n_rows = m * TOP_K + N_EXPERTS * FFN_TM
    n_tiles = n_rows // FFN_TM
    experts = jnp.arange(N_EXPERTS, dtype=jnp.int32)
    pos8 = jnp.sum(jnp.where(eid8[:, :, None] == experts, offs, 0), axis=-1) + rank8
    tile_start = jnp.arange(n_tiles, dtype=jnp.int32) * FFN_TM
    tile_expert = jnp.minimum(jnp.sum((ends[None, :] <= tile_start[:, None]).astype(jnp.int32), axis=1),
                              N_EXPERTS - 1)
    n_used = (ends[-1:] // FFN_TM).astype(jnp.int32)
    xs = moe_dispatch(hpk, pos8, n_rows)
    ys = moe_experts(xs, tile_expert, n_used, w_gate.astype(BF16), w_up.astype(BF16), w_down.astype(BF16))
    return moe_combine(ys, pos8, gate8.T, h32, h16, s_gate.astype(BF16), s_up.astype(BF16),
                       s_down.astype(BF16), ln_g, ln_b)


def _even_mixer(h16, lb, in_proj, out_proj, hg_gn_w, rw_mix, rw_w0, rw_w2, rw_a0, rw_a2, rw_g2, rw_kk, rw_ka,
                rw_rk, rw_ln_w, rw_ln_b, batch, seq):
    hg_cols = 4 * HG_W
    main_end = hg_cols + 3 * RW_W
    c1 = main_end + RW_DECAY_RANK
    c2 = c1 + RW_A_RANK
    padc = lambda w, n: jnp.pad(w, ((0, 0), (0, n - w.shape[1])))
    w_low = jnp.concatenate([padc(in_proj[:, main_end:c1], 128), padc(in_proj[:, c1:c2], 128),
                             padc(in_proj[:, c2:], 256)], axis=1)
    p_hg = matmul(h16, in_proj[:, :hg_cols].astype(BF16), F32)
    p_main = matmul(h16, in_proj[:, hg_cols:main_end].astype(BF16), F32)
    p_low = matmul(h16, w_low.astype(BF16), F32)
    o_a = hgrn2(p_hg, lb, hg_gn_w, batch, seq)
    o_b, gate_b = rwkv7(p_main, p_low, rw_mix, rw_w0, rw_w2, rw_a0, rw_a2, rw_g2, rw_kk, rw_ka, rw_rk, rw_ln_w,
                        rw_ln_b, batch, seq)
    return o_a, o_b, gate_b


def _odd_mixer(h16, in_proj, conv_w, conv_b, dt_bias, a_log, d_skip, norm_w, batch, seq):
    c1 = SSD_INNER
    c2 = c1 + SSD_CONV_DIM
    z = matmul(h16, in_proj[:, :c1].astype(BF16), BF16)
    xbc = matmul(h16, in_proj[:, c1:c2].astype(BF16), F32)
    dt = matmul(h16, jnp.pad(in_proj[:, c2:], ((0, 0), (0, 128 - SSD_HEADS))).astype(BF16), F32)
    return (ssd(z, xbc, dt, conv_w, conv_b, dt_bias, a_log, d_skip, norm_w, batch, seq),)


def kernel(x, hg_lb, ev_in_proj, ev_out_proj, hg_gn_w, rw_mix, rw_w0, rw_w2, rw_a0, rw_a2, rw_g2, rw_kk, rw_ka,
           rw_rk, rw_ln_w, rw_ln_b, od_in_proj, od_conv_w, od_conv_b, od_dt_bias, od_a_log, od_d, od_norm_w,
           od_out_proj, moe_router, moe_bias, moe_w_gate, moe_w_up, moe_w_down, sh_w_gate, sh_w_up, sh_w_down,
           ln1_g, ln1_b, ln2_g, ln2_b):
    batch, seq, d = x.shape
    lbs = jnp.cumsum(jax.nn.softmax(hg_lb.astype(F32), axis=0), axis=0)
    h32 = x.reshape(batch * seq, d)
    h16 = h32.astype(BF16)
    for l in range(DEPTH):
        if l % 2 == 0:
            e = l // 2
            mix = _even_mixer(h16, lbs[l], ev_in_proj[e], ev_out_proj[e], hg_gn_w[e], rw_mix[e], rw_w0[e],
                              rw_w2[e], rw_a0[e], rw_a2[e], rw_g2[e], rw_kk[e], rw_ka[e], rw_rk[e],
                              rw_ln_w[e], rw_ln_b[e], batch, seq)
            w_out = ev_out_proj[e]
        else:
            o = l // 2
            mix = _odd_mixer(h16, od_in_proj[o], od_conv_w[o], od_conv_b[o], od_dt_bias[o], od_a_log[o],
                             od_d[o], od_norm_w[o], batch, seq)
            w_out = od_out_proj[o]
        h32, h16, hpk = matmul_residual_ln(mix, w_out.astype(BF16), h32, ln1_g[l], ln1_b[l])
        h32, h16 = moe_layer(h32, h16, hpk, moe_router[l], moe_bias[l], moe_w_gate[l], moe_w_up[l],
                             moe_w_down[l], sh_w_gate[l], sh_w_up[l], sh_w_down[l], ln2_g[l], ln2_b[l])
    return h32.reshape(batch, seq, d)
```

```python
import functools
import math

import jax
import jax.numpy as jnp
import numpy as np
from jax import lax
from jax.experimental import pallas as pl
from jax.experimental.pallas import tpu as pltpu

F32 = jnp.float32
BF16 = jnp.bfloat16

D_MODEL = 2048
DEPTH = 2
CHUNK = 64

HG_HEADS = 8
HG_D = 128
HG_W = HG_HEADS * HG_D
RW_HEADS = 16
RW_N = 64
RW_W = RW_HEADS * RW_N
RW_DECAY_RANK = 64
RW_A_RANK = 64
RW_GATE_RANK = 160
RW_GN_EPS = 64e-5
RW_LOW_PAD = 512

SSD_INNER = 2 * D_MODEL
SSD_P = 64
SSD_HEADS = SSD_INNER // SSD_P
SSD_GROUPS = 8
SSD_HPG = SSD_HEADS // SSD_GROUPS
SSD_N = 128
SSD_CONV = 4
SSD_BC = SSD_GROUPS * SSD_N
SSD_CONV_DIM = SSD_INNER + 2 * SSD_BC
SSD_GW = SSD_HPG * SSD_P

N_EXPERTS = 64
TOP_K = 8
N_GROUPS = 8
TOPK_GROUPS = 4
EXPERT_DIM = 512
ROUTED_SCALE = 2.5

NORM_EPS = 1e-5
DN_ALPHA = (2 * DEPTH) ** 0.25

VMEM_LIMIT = 56 * 1024 * 1024


def _cparams(*sem):
    return pltpu.CompilerParams(dimension_semantics=sem, vmem_limit_bytes=VMEM_LIMIT)


def _const_spec(shape):
    nd = len(shape)
    return pl.BlockSpec(shape, lambda *_: (0,) * nd, pipeline_mode=pl.Buffered(1))


def _split3(x):
    hi = x.astype(BF16)
    r1 = x - hi.astype(F32)
    mid = r1.astype(BF16)
    lo = (r1 - mid.astype(F32)).astype(BF16)
    return hi, mid, lo


def _dot(a, b):
    return jnp.dot(a, b, preferred_element_type=F32)


def _dot_exact_lhs(m_bf16, x):
    hi, mid, lo = _split3(x)
    return _dot(m_bf16, hi) + _dot(m_bf16, mid) + _dot(m_bf16, lo)


def _dot_exact_rhs(x, m_bf16):
    hi, mid, lo = _split3(x)
    return _dot(hi, m_bf16) + _dot(mid, m_bf16) + _dot(lo, m_bf16)


def _sigmoid(x):
    return 1.0 / (1.0 + jnp.exp(-x))


def _silu(x):
    return x * _sigmoid(x)


def _softplus(x):
    return jnp.maximum(x, 0.0) + jnp.log(1.0 + jnp.exp(-jnp.abs(x)))


def _mm_kernel(x_ref, w_ref, o_ref):
    o_ref[...] = _dot(x_ref[...], w_ref[...]).astype(o_ref.dtype)


def matmul(x, w, out_dtype, tm=1024, tn=512):
    m, k = x.shape
    n = w.shape[1]
    tm = min(tm, m)
    tn = min(tn, n)
    assert m % tm == 0 and n % tn == 0
    return pl.pallas_call(
        _mm_kernel,
        grid=(m // tm, n // tn),
        in_specs=[pl.BlockSpec((tm, k), lambda i, j: (i, 0)),
                  pl.BlockSpec((k, tn), lambda i, j: (0, j))],
        out_specs=pl.BlockSpec((tm, tn), lambda i, j: (i, j)),
        out_shape=jax.ShapeDtypeStruct((m, n), out_dtype),
        compiler_params=_cparams("parallel", "arbitrary"),
        name="matmul",
    )(x, w)


def _layer_norm_rows(y, g, b):
    mu = jnp.mean(y, axis=-1, keepdims=True)
    yc = y - mu
    var = jnp.mean(yc * yc, axis=-1, keepdims=True)
    return yc * lax.rsqrt(var + NORM_EPS) * g + b


def _bf16_bits(x):
    u = lax.bitcast_convert_type(x, jnp.uint32)
    u = u + jnp.uint32(0x7FFF) + ((u >> 16) & jnp.uint32(1))
    return u & jnp.uint32(0xFFFF0000)


def _pack_halves(lo, hi):
    return (_bf16_bits(lo) >> 16) | _bf16_bits(hi)


def _unpack_halves(u):
    lo = lax.bitcast_convert_type(u << 16, F32)
    hi = lax.bitcast_convert_type(u & jnp.uint32(0xFFFF0000), F32)
    return lo, hi


HALF = D_MODEL // 2
ROW_SUB = 8
ROW_LANES = HALF // ROW_SUB


def _store_row_tiles(ref, packed):
    n = packed.shape[0]
    for c in range(ROW_SUB):
        ref[pl.ds(c, n, stride=ROW_SUB), :] = packed[:, c * ROW_LANES:(c + 1) * ROW_LANES]


def _load_row_tiles(ref, n):
    return jnp.concatenate([ref[pl.ds(c, n, stride=ROW_SUB), :] for c in range(ROW_SUB)], axis=1)


def _ln_outputs(y, g_ref, b_ref, o32_ref, o16_ref, opk_ref):
    o = _layer_norm_rows(y, g_ref[...], b_ref[...])
    o32_ref[...] = o
    o16_ref[...] = o.astype(BF16)
    _store_row_tiles(opk_ref, _pack_halves(o[:, :HALF], o[:, HALF:]))


def _mm_ln_kernel(x_ref, w_ref, h_ref, g_ref, b_ref, o32_ref, o16_ref, opk_ref):
    y = _dot(x_ref[...], w_ref[...]) + DN_ALPHA * h_ref[...]
    _ln_outputs(y, g_ref, b_ref, o32_ref, o16_ref, opk_ref)


def _mm2_ln_kernel(xa_ref, xb_ref, gate_ref, w_ref, h_ref, g_ref, b_ref, o32_ref, o16_ref, opk_ref):
    ka = xa_ref.shape[1]
    xb = (xb_ref[...] * gate_ref[...]).astype(BF16)
    y = _dot(xa_ref[...], w_ref[:ka, :]) + _dot(xb, w_ref[ka:, :]) + DN_ALPHA * h_ref[...]
    _ln_outputs(y, g_ref, b_ref, o32_ref, o16_ref, opk_ref)


def matmul_residual_ln(xs, w, h, g, b, tm=256):
    m = xs[0].shape[0]
    k, d = w.shape
    tm = min(tm, m)
    assert m % tm == 0
    row = lambda a: pl.BlockSpec((tm, a.shape[1]), lambda i: (i, 0))
    return pl.pallas_call(
        _mm_ln_kernel if len(xs) == 1 else _mm2_ln_kernel,
        grid=(m // tm,),
        in_specs=[row(a) for a in xs] + [_const_spec((k, d)), row(h), _const_spec((1, d)), _const_spec((1, d))],
        out_specs=[pl.BlockSpec((tm, d), lambda i: (i, 0)),
                   pl.BlockSpec((tm, d), lambda i: (i, 0)),
                   pl.BlockSpec((tm * ROW_SUB, ROW_LANES), lambda i: (i, 0))],
        out_shape=[jax.ShapeDtypeStruct((m, d), F32), jax.ShapeDtypeStruct((m, d), BF16),
                   jax.ShapeDtypeStruct((m * ROW_SUB, ROW_LANES), jnp.uint32)],
        compiler_params=_cparams("parallel"),
        name="matmul_residual_ln",
    )(*xs, w, h, g.reshape(1, d), b.reshape(1, d))


HG_LEVELS = (32, 16, 8, 4, 2, 1)


def _hgrn2_constants():
    t = np.arange(CHUNK)[:, None]
    u = np.arange(CHUNK)[None, :]
    blocks = [(u <= t)]
    for b in HG_LEVELS:
        start = (t // b) * b
        end = start + b - 1
        blocks.append((u >= start) & (u <= t))
        blocks.append((u > t) & (u <= end))
    blocks.append(u > t)
    cm = np.concatenate(blocks, axis=0).astype(np.float32)
    s = u
    masks = [((t // (2 * b)) == (s // (2 * b))) & ((t // b) % 2 == 1) & ((s // b) % 2 == 0) for b in HG_LEVELS]
    return cm, np.stack(masks).astype(np.float32)


HG_HPS = 4


def _hgrn2_kernel(q_ref, f_ref, i_ref, g_ref, lb_ref, gnw_ref, cm_ref, mask_ref, o_ref, st_ref, *, n_chunks):
    @pl.when(pl.program_id(2) == 0)
    def _():
        st_ref[...] = jnp.zeros_like(st_ref)

    lb = lb_ref[...]
    cm = cm_ref[...]
    nt = (((1,), (1,)), ((), ()))
    tail = CHUNK + 2 * CHUNK * len(HG_LEVELS)

    def chunk(c, carry):
        rows = pl.ds(pl.multiple_of(c * CHUNK, CHUNK), CHUNK)
        fp_all = f_ref[rows, :]
        lf_all = jnp.log(lb + (1.0 - lb) * _sigmoid(fp_all))
        k_all = (1.0 - lb) * _sigmoid(-fp_all)
        seg_all = _dot_exact_lhs(cm, lf_all)
        for hh in range(HG_HPS):
            cols = slice(hh * HG_D, (hh + 1) * HG_D)
            q = q_ref[rows, cols]
            v = i_ref[rows, cols]
            g = g_ref[rows, cols]
            k = k_all[:, cols]
            seg = seg_all[:, cols]
            bc = seg[0:CHUNK]
            st = st_ref[hh]
            o = lax.dot_general((q * jnp.exp(bc)).astype(BF16), st.astype(BF16), nt, preferred_element_type=F32)
            att = jnp.zeros((CHUNK, CHUNK), F32)
            for li in range(len(HG_LEVELS)):
                base = CHUNK + 2 * CHUNK * li
                qs = q * jnp.exp(seg[base:base + CHUNK])
                ks = k * jnp.exp(seg[base + CHUNK:base + 2 * CHUNK])
                a_l = lax.dot_general(qs.astype(BF16), ks.astype(BF16), nt, preferred_element_type=F32)
                att = att + a_l * mask_ref[li]
            o = o + _dot(att.astype(BF16), v.astype(BF16)) + jnp.sum(q * k, axis=-1, keepdims=True) * v
            kd = k * jnp.exp(seg[tail:tail + CHUNK])
            st_ref[hh] = st * jnp.exp(bc[CHUNK - 1:CHUNK]) + _dot(v.T.astype(BF16), kd.astype(BF16))
            ms = jnp.mean(o * o, axis=-1, keepdims=True)
            o_ref[rows, cols] = (o * lax.rsqrt(ms + NORM_EPS) * gnw_ref[:, cols] * _silu(g)).astype(o_ref.dtype)
        return carry

    lax.fori_loop(0, n_chunks, chunk, 0)


def hgrn2(p_hg, lb, gn_w, batch, seq, tb=256):
    tb = min(tb, seq)
    nt = seq // tb
    hg = HG_HEADS // HG_HPS
    w = HG_HPS * HG_D
    cm, masks = _hgrn2_constants()
    spec = lambda off: pl.BlockSpec((tb, w), lambda b, h, t: (b * nt + t, off * hg + h))
    return pl.pallas_call(
        functools.partial(_hgrn2_kernel, n_chunks=tb // CHUNK),
        grid=(batch, hg, nt),
        in_specs=[spec(0), spec(1), spec(2), spec(3),
                  pl.BlockSpec((1, w), lambda b, h, t: (0, h)),
                  pl.BlockSpec((1, w), lambda b, h, t: (0, h)),
                  _const_spec(cm.shape), _const_spec(masks.shape)],
        out_specs=pl.BlockSpec((tb, w), lambda b, h, t: (b * nt + t, h)),
        out_shape=jax.ShapeDtypeStruct((batch * seq, HG_W), BF16),
        scratch_shapes=[pltpu.VMEM((HG_HPS, HG_D, HG_D), F32)],
        compiler_params=_cparams("parallel", "parallel", "arbitrary"),
        name="hgrn2",
    )(p_hg, p_hg, p_hg, p_hg, lb.reshape(1, HG_W), gn_w.reshape(1, HG_W),
      jnp.asarray(cm, BF16), jnp.asarray(masks))


def _rwkv_prep_kernel(pm_ref, pmprev_ref, plo_ref, ploprev_ref, mixm_ref, mixl_ref, w0_ref, w2_ref, a0_ref,
                      a2_ref, g2_ref, r_o, k_o, v_o, w_o, a_o, g_o, *, tb, seq):
    first = (pl.program_id(0) * tb) % seq == 0
    keep = jnp.where(first, 0.0, 1.0)

    def shift_mix(cur, prev_blk, mix):
        row = lax.broadcasted_iota(jnp.int32, cur.shape, 0)
        shifted = jnp.where(row == 0, prev_blk[7:8, :] * keep, pltpu.roll(cur, 1, 0))
        return cur + (shifted - cur) * mix

    for idx, out in enumerate((r_o, k_o, v_o)):
        cols = slice(idx * RW_W, (idx + 1) * RW_W)
        out[...] = shift_mix(pm_ref[:, cols], pmprev_ref[:, cols], mixm_ref[:, cols])
    low = shift_mix(plo_ref[...], ploprev_ref[...], mixl_ref[...])
    wl = jnp.tanh(low[:, 0:128]).astype(BF16)
    al = low[:, 128:256].astype(BF16)
    gl = _sigmoid(low[:, 256:512]).astype(BF16)
    w_log = -_softplus(-(w0_ref[...] + _dot(wl, w2_ref[...]))) - 0.5
    w_o[...] = jnp.exp(-jnp.exp(w_log))
    a_o[...] = _sigmoid(a0_ref[...] + _dot(al, a2_ref[...]))
    g_o[...] = _dot(gl, g2_ref[...])


def rwkv_prep(p_main, p_low, mix_main, mix_low, w0, w2p, a0, a2p, g2p, seq, tb=256):
    m = p_main.shape[0]
    tb = min(tb, seq)
    prev = lambda i: (jnp.maximum(i * (tb // 8) - 1, 0), 0)
    out = jax.ShapeDtypeStruct((m, RW_W), F32)
    ospec = pl.BlockSpec((tb, RW_W), lambda i: (i, 0))
    return pl.pallas_call(
        functools.partial(_rwkv_prep_kernel, tb=tb, seq=seq),
        grid=(m // tb,),
        in_specs=[pl.BlockSpec((tb, 3 * RW_W), lambda i: (i, 0)),
                  pl.BlockSpec((8, 3 * RW_W), prev),
                  pl.BlockSpec((tb, RW_LOW_PAD), lambda i: (i, 0)),
                  pl.BlockSpec((8, RW_LOW_PAD), prev),
                  _const_spec((1, 3 * RW_W)), _const_spec((1, RW_LOW_PAD)),
                  _const_spec((1, RW_W)), _const_spec((128, RW_W)),
                  _const_spec((1, RW_W)), _const_spec((128, RW_W)),
                  _const_spec((256, RW_W))],
        out_specs=[ospec] * 6,
        out_shape=[out] * 6,
        compiler_params=_cparams("parallel"),
        name="rwkv_prep",
    )(p_main, p_main, p_low, p_low, mix_main, mix_low, w0, w2p, a0, a2p, g2p)


def _rwkv_scan_kernel(r_ref, k_ref, v_ref, w_ref, a_ref, kkp_ref, kap_ref, rkp_ref, lnw_ref, lnb_ref,
                      o_ref, s_ref, vec_ref, *, tt):
    @pl.when(pl.program_id(0) == 0)
    def _():
        s_ref[...] = jnp.zeros_like(s_ref)

    kkp = kkp_ref[...]
    kap = kap_ref[...]
    rkp = rkp_ref[...]
    lnw = lnw_ref[...]
    lnb = lnb_ref[...]

    def step(t, carry):
        kt = k_ref[t]
        at = a_ref[t]
        rt = r_ref[t]
        vt = v_ref[t]
        kk = kt * kkp
        nrm = jnp.sqrt(jnp.sum(kk * kk, axis=0, keepdims=True))
        kk = kk / jnp.maximum(nrm, 1e-12)
        kh = kt * (1.0 + (at - 1.0) * kap)
        vec_ref[0] = -kk
        vec_ref[1] = kk * at
        vec_ref[2] = kh
        sa = jnp.zeros((RW_N, kt.shape[1]), F32)
        for j in range(RW_N):
            sa = sa + s_ref[j] * vec_ref[0, j:j + 1, :]
        y = jnp.zeros_like(sa)
        for j in range(RW_N):
            sj = (s_ref[j] * w_ref[t, j:j + 1, :] + sa * vec_ref[1, j:j + 1, :]
                  + vt * vec_ref[2, j:j + 1, :])
            s_ref[j] = sj
            y = y + sj * r_ref[t, j:j + 1, :]
        mu = jnp.mean(y, axis=0, keepdims=True)
        yc = y - mu
        var = jnp.mean(yc * yc, axis=0, keepdims=True)
        yn = yc * lax.rsqrt(var + RW_GN_EPS) * lnw + lnb
        bonus = jnp.sum(rt * kh * rkp, axis=0, keepdims=True) * vt
        o_ref[t] = yn + bonus
        return carry

    lax.fori_loop(0, tt, step, 0)


def rwkv_scan(r, k, v, w, a, kkp, kap, rkp, lnw, lnb, tt=16):
    t, n, lanes = r.shape
    tt = min(tt, t)
    blk = pl.BlockSpec((tt, n, lanes), lambda i: (i, 0, 0))
    par = _const_spec((n, lanes))
    return pl.pallas_call(
        functools.partial(_rwkv_scan_kernel, tt=tt),
        grid=(t // tt,),
        in_specs=[blk] * 5 + [par] * 5,
        out_specs=blk,
        out_shape=jax.ShapeDtypeStruct((t, n, lanes), F32),
        scratch_shapes=[pltpu.VMEM((n, n, lanes), F32), pltpu.VMEM((3, n, lanes), F32)],
        compiler_params=_cparams("arbitrary"),
        name="rwkv_scan",
    )(r, k, v, w, a, kkp, kap, rkp, lnw, lnb)


def _time_major(x, batch, seq):
    return x.reshape(batch, seq, RW_HEADS, RW_N).transpose(1, 3, 0, 2).reshape(seq, RW_N, batch * RW_HEADS)


def _token_major(x, batch, seq):
    return x.reshape(seq, RW_N, batch, RW_HEADS).transpose(2, 0, 3, 1).reshape(batch * seq, RW_W)


def _head_param(p, batch):
    return jnp.tile(p.reshape(RW_HEADS, RW_N).T, (1, batch))


def _pad_rows(w, rows):
    return jnp.pad(w, ((0, rows - w.shape[0]), (0, 0)))


def rwkv7(p_main, p_low, mix, w0, w2, a0, a2, g2, k_k, k_a, r_k, ln_w, ln_b, batch, seq):
    mix_main = mix[:3 * RW_W].reshape(1, -1)
    lo = mix[3 * RW_W:]
    c1 = RW_DECAY_RANK
    c2 = c1 + RW_A_RANK
    mix_low = jnp.concatenate([jnp.pad(lo[:c1], (0, 128 - RW_DECAY_RANK)),
                               jnp.pad(lo[c1:c2], (0, 128 - RW_A_RANK)),
                               jnp.pad(lo[c2:], (0, 256 - RW_GATE_RANK))]).reshape(1, RW_LOW_PAD)
    r, k, v, w, a, g = rwkv_prep(
        p_main, p_low, mix_main, mix_low, w0.reshape(1, -1), _pad_rows(w2, 128).astype(BF16),
        a0.reshape(1, -1), _pad_rows(a2, 128).astype(BF16), _pad_rows(g2, 256).astype(BF16), seq)
    tm = lambda x: _time_major(x, batch, seq)
    hp = lambda p: _head_param(p.reshape(-1), batch)
    o = rwkv_scan(tm(r), tm(k), tm(v), tm(w), tm(a), hp(k_k), hp(k_a), hp(r_k), hp(ln_w), hp(ln_b))
    return _token_major(o, batch, seq), g


def _ssd_kernel(z_ref, xbc_ref, dt_ref, cw_ref, cb_ref, dtb_ref, alog_ref, dexp_ref, nw_ref, ex_ref, tri_ref,
                o_ref, xpad_ref, xact_ref, st_ref, y_ref):
    c = pl.program_id(1)

    @pl.when(c == 0)
    def _():
        st_ref[...] = jnp.zeros_like(st_ref)
        xpad_ref[0:8, :] = jnp.zeros((8, SSD_CONV_DIM), F32)

    xpad_ref[8:8 + CHUNK, :] = xbc_ref[...]
    ct = 512
    for j in range(SSD_CONV_DIM // ct):
        cols = slice(j * ct, (j + 1) * ct)
        win = xpad_ref[:, cols]
        acc = cw_ref[0:1, cols] * win
        for tap in range(1, SSD_CONV):
            acc = pltpu.roll(acc, 1, 0) + cw_ref[tap:tap + 1, cols] * win
        xact_ref[:, cols] = _silu(acc[8:] + cb_ref[:, cols])
    xpad_ref[0:8, :] = xpad_ref[CHUNK:CHUNK + 8, :]

    ex = ex_ref[...]
    dtv = _softplus(dt_ref[...] + dtb_ref[...])
    da = dtv * (-jnp.exp(alog_ref[...]))
    acum = _dot_exact_lhs(tri_ref[...], da)
    acum_t = acum.T
    dtx = _dot_exact_rhs(dtv, ex)
    acx = _dot_exact_rhs(acum, ex)
    alx = acx[CHUNK - 1:CHUNK, :]
    e_in = jnp.exp(acx)
    e_out = jnp.exp(alx - acx)
    e_last = jnp.exp(alx)
    xs = xact_ref[:, 0:SSD_INNER]
    xdt = xs * dtx
    ti = lax.broadcasted_iota(jnp.int32, (CHUNK, CHUNK), 0)
    si = lax.broadcasted_iota(jnp.int32, (CHUNK, CHUNK), 1)
    causal = ti >= si
    lane = lax.broadcasted_iota(jnp.int32, (CHUNK, 128), 1)
    nt = (((1,), (1,)), ((), ()))

    for g in range(SSD_GROUPS):
        bg = xact_ref[:, SSD_INNER + g * SSD_N:SSD_INNER + (g + 1) * SSD_N]
        cg = xact_ref[:, SSD_INNER + SSD_BC + g * SSD_N:SSD_INNER + SSD_BC + (g + 1) * SSD_N]
        cgb = cg.astype(BF16)
        cb = lax.dot_general(cgb, bg.astype(BF16), nt, preferred_element_type=F32)
        gc = slice(g * SSD_GW, (g + 1) * SSD_GW)
        sgt = st_ref[g]
        y_g = _dot(cgb, sgt.astype(BF16)) * e_in[:, gc]
        xdt_g = xdt[:, gc]
        for m in range(SSD_HPG // 2):
            xp = xdt_g[:, m * 128:(m + 1) * 128]
            yp = y_g[:, m * 128:(m + 1) * 128]
            for half in range(2):
                h = g * SSD_HPG + 2 * m + half
                seg = acum[:, h:h + 1] - acum_t[h:h + 1, :]
                lmat = jnp.where(causal, jnp.exp(jnp.minimum(seg, 0.0)), 0.0)
                rhs = jnp.where((lane >= 64) == (half == 1), xp, 0.0)
                yp = yp + _dot((cb * lmat).astype(BF16), rhs.astype(BF16))
            y_ref[:, g * SSD_GW + m * 128:g * SSD_GW + (m + 1) * 128] = yp
        st_ref[g] = sgt * e_last[:, gc] + _dot(bg.T.astype(BF16), (xdt_g * e_out[:, gc]).astype(BF16))

    for g in range(SSD_GROUPS):
        gc = slice(g * SSD_GW, (g + 1) * SSD_GW)
        y = y_ref[:, gc] + dexp_ref[:, gc] * xact_ref[:, gc]
        y = y * _silu(z_ref[:, gc].astype(F32))
        ms = jnp.mean(y * y, axis=-1, keepdims=True)
        o_ref[:, gc] = (y * lax.rsqrt(ms + NORM_EPS) * nw_ref[:, gc]).astype(o_ref.dtype)


def ssd(z, xbc, dt, conv_w, conv_b, dt_bias, a_log, d_skip, norm_w, batch, seq):
    nc = seq // CHUNK
    pad = lambda v: jnp.pad(v.reshape(1, -1), ((0, 0), (0, 128 - SSD_HEADS)))
    ex = np.zeros((128, SSD_INNER), np.float32)
    for h in range(SSD_HEADS):
        ex[h, h * SSD_P:(h + 1) * SSD_P] = 1.0
    tri = np.tril(np.ones((CHUNK, CHUNK), np.float32))
    row = lambda w: pl.BlockSpec((CHUNK, w), lambda b, c: (b * nc + c, 0))
    return pl.pallas_call(
        _ssd_kernel,
        grid=(batch, nc),
        in_specs=[row(SSD_INNER), row(SSD_CONV_DIM), row(128),
                  _const_spec((SSD_CONV, SSD_CONV_DIM)), _const_spec((1, SSD_CONV_DIM)),
                  _const_spec((1, 128)), _const_spec((1, 128)),
                  _const_spec((1, SSD_INNER)), _const_spec((1, SSD_INNER)),
                  _const_spec((128, SSD_INNER)), _const_spec((CHUNK, CHUNK))],
        out_specs=row(SSD_INNER),
        out_shape=jax.ShapeDtypeStruct((batch * seq, SSD_INNER), BF16),
        scratch_shapes=[pltpu.VMEM((CHUNK + 8, SSD_CONV_DIM), F32),
                        pltpu.VMEM((CHUNK, SSD_CONV_DIM), F32),
                        pltpu.VMEM((SSD_GROUPS, SSD_N, SSD_GW), F32),
                        pltpu.VMEM((CHUNK, SSD_INNER), F32)],
        compiler_params=_cparams("parallel", "arbitrary"),
        name="ssd",
    )(z, xbc, dt, conv_w, conv_b.reshape(1, -1), pad(dt_bias), pad(a_log),
      jnp.repeat(d_skip, SSD_P).reshape(1, -1), norm_w.reshape(1, -1), jnp.asarray(ex, BF16),
      jnp.asarray(tri, BF16))


FFN_TM = 256


def _router_kernel(x_ref, rt_ref, bias_ref, up_ref, ls_ref, eid_o, rank_o, gate_o, cnt_o, carry_ref, *, tb):
    @pl.when(pl.program_id(0) == 0)
    def _():
        carry_ref[...] = jnp.zeros_like(carry_ref)

    nt = (((1,), (1,)), ((), ()))
    x = x_ref[...]
    rt = rt_ref[...]
    xh = x.astype(BF16)
    xl = (x - xh.astype(F32)).astype(BF16)
    rh = rt.astype(BF16)
    rl = (rt - rh.astype(F32)).astype(BF16)
    dg = lambda a, b: lax.dot_general(a, b, nt, preferred_element_type=F32)
    logits = dg(rh, xh) + dg(rh, xl) + dg(rl, xh)
    scores = _sigmoid(logits)
    reps = tb // 128
    wide = lambda a: jnp.concatenate([a] * reps, axis=1) if reps > 1 else a
    biased = scores + wide(bias_ref[...])
    neg = -jnp.inf

    io8 = lax.broadcasted_iota(jnp.int32, (8, tb), 0)
    blocks, gs = [], []
    for g in range(N_GROUPS):
        blk = biased[8 * g:8 * g + 8, :]
        m1 = jnp.max(blk, axis=0, keepdims=True)
        first = jnp.min(jnp.where(blk == m1, io8, 8), axis=0, keepdims=True)
        m2 = jnp.max(jnp.where(io8 == first, neg, blk), axis=0, keepdims=True)
        blocks.append(blk)
        gs.append(m1 + m2)
    masked = []
    for g in range(N_GROUPS):
        ahead = jnp.zeros((1, tb), jnp.int32)
        for o in range(N_GROUPS):
            if o == g:
                continue
            beats = (gs[o] > gs[g]) | ((gs[o] == gs[g]) & (o < g))
            ahead = ahead + jnp.where(beats, 1, 0)
        masked.append(jnp.where(ahead < TOPK_GROUPS, blocks[g], neg))
    masked = jnp.concatenate(masked, axis=0)
    eidx = lax.broadcasted_iota(jnp.int32, (N_EXPERTS, tb), 0)
    ahead = jnp.zeros((N_EXPERTS, tb), jnp.int32)
    for o in range(N_EXPERTS):
        row = masked[o:o + 1, :]
        beats = (row > masked) | ((row == masked) & (eidx > o))
        ahead = ahead + jnp.where(beats, 1, 0)
    sel = ahead < TOP_K
    self_ = jnp.where(sel, 1.0, 0.0)
    wts = jnp.where(sel, scores, 0.0)
    gate = wts / jnp.sum(wts, axis=0, keepdims=True) * ROUTED_SCALE

    selb = self_.astype(BF16)
    carry = carry_ref[...]
    rank = _dot(selb, up_ref[...]) + wide(carry)
    new_carry = carry + _dot(selb, jnp.ones((tb, 128), BF16))
    carry_ref[...] = new_carry
    cnt_o[...] = new_carry
    before = _dot(ls_ref[...], selb)
    eidf = eidx.astype(F32)
    for j in range(TOP_K):
        hit = sel & (before == float(j))
        pick = lambda a: jnp.sum(jnp.where(hit, a, 0.0), axis=0, keepdims=True)
        eid_o[j:j + 1, :] = pick(eidf).astype(jnp.int32)
        rank_o[j:j + 1, :] = pick(rank).astype(jnp.int32)
        gate_o[j:j + 1, :] = pick(gate)


def moe_route(h32, router, bias, tb=256):
    m = h32.shape[0]
    tb = min(tb, m)
    up = np.triu(np.ones((tb, tb), np.float32), k=1)
    ls = np.tril(np.ones((N_EXPERTS, N_EXPERTS), np.float32), k=-1)
    slot = lambda dt: jax.ShapeDtypeStruct((TOP_K, m), dt)
    return pl.pallas_call(
        functools.partial(_router_kernel, tb=tb),
        grid=(m // tb,),
        in_specs=[pl.BlockSpec((tb, D_MODEL), lambda i: (i, 0)),
                  _const_spec((N_EXPERTS, D_MODEL)), _const_spec((N_EXPERTS, 128)),
                  _const_spec((tb, tb)), _const_spec((N_EXPERTS, N_EXPERTS))],
        out_specs=[pl.BlockSpec((TOP_K, tb), lambda i: (0, i))] * 3
                  + [pl.BlockSpec((N_EXPERTS, 128), lambda i: (0, 0))],
        out_shape=[slot(jnp.int32), slot(jnp.int32), slot(F32),
                   jax.ShapeDtypeStruct((N_EXPERTS, 128), F32)],
        scratch_shapes=[pltpu.VMEM((N_EXPERTS, 128), F32)],
        compiler_params=_cparams("arbitrary"),
        name="moe_route",
    )(h32, router.T, jnp.broadcast_to(bias.reshape(N_EXPERTS, 1), (N_EXPERTS, 128)),
      jnp.asarray(up, BF16), jnp.asarray(ls, BF16))


def _frame_rows(p):
    return pl.ds(pl.multiple_of(p * ROW_SUB, ROW_SUB), ROW_SUB)


def _dispatch_kernel(seg_ref, ends_ref, pos_ref, x_ref, xs_out, zero_ref, sem, zsem, *, tb):
    tile_rows = FFN_TM * ROW_SUB

    @pl.when(pl.program_id(0) == 0)
    def _():
        zero_ref[...] = jnp.zeros_like(zero_ref)

        def fill(e):
            start = pl.multiple_of((ends_ref[e] - FFN_TM) * ROW_SUB, tile_rows)
            return pltpu.make_async_copy(zero_ref, xs_out.at[pl.ds(start, tile_rows), :], zsem)

        def start_fill(e, c):
            @pl.when(seg_ref[e] > 0)
            def _():
                fill(e).start()
            return c

        def wait_fill(e, c):
            @pl.when(seg_ref[e] > 0)
            def _():
                fill(e).wait()
            return c

        lax.fori_loop(0, N_EXPERTS, start_fill, 0)
        lax.fori_loop(0, N_EXPERTS, wait_fill, 0)

    def copy(t, p):
        return pltpu.make_async_copy(x_ref.at[_frame_rows(t), :], xs_out.at[_frame_rows(p), :], sem)

    def issue(t, c):
        for j in range(TOP_K):
            copy(t, pos_ref[j, t]).start()
        return c

    lax.fori_loop(0, tb, issue, 0)

    def drain(t, c):
        for j in range(TOP_K):
            copy(0, 0).wait()
        return c

    lax.fori_loop(0, tb, drain, 0)


def moe_dispatch(xpk, pos8, seg, ends, n_rows, tb=256):
    m = xpk.shape[0] // ROW_SUB
    tb = min(tb, m)
    grid_spec = pltpu.PrefetchScalarGridSpec(
        num_scalar_prefetch=2,
        grid=(m // tb,),
        in_specs=[pl.BlockSpec((TOP_K, tb), lambda i, sg, en: (0, i), memory_space=pltpu.SMEM),
                  pl.BlockSpec((tb * ROW_SUB, ROW_LANES), lambda i, sg, en: (i, 0))],
        out_specs=pl.BlockSpec(memory_space=pl.ANY),
        scratch_shapes=[pltpu.VMEM((FFN_TM * ROW_SUB, ROW_LANES), xpk.dtype),
                        pltpu.SemaphoreType.DMA(()), pltpu.SemaphoreType.DMA(())],
    )
    return pl.pallas_call(
        functools.partial(_dispatch_kernel, tb=tb),
        grid_spec=grid_spec,
        out_shape=jax.ShapeDtypeStruct((n_rows * ROW_SUB, ROW_LANES), xpk.dtype),
        compiler_params=_cparams("arbitrary"),
        name="moe_dispatch",
    )(seg, ends, pos8, xpk)


def _ffn_kernel(te_ref, nu_ref, xs_ref, wg_ref, wu_ref, wd_ref, ys_ref, wg16_ref, wu16_ref, wd16_ref):
    i = pl.program_id(0)

    @pl.when((i == 0) | (te_ref[i] != te_ref[jnp.maximum(i - 1, 0)]))
    def _():
        wg16_ref[...] = wg_ref[0].astype(BF16)
        wu16_ref[...] = wu_ref[0].astype(BF16)
        wd16_ref[...] = wd_ref[0].astype(BF16)

    @pl.when(i < nu_ref[0])
    def _():
        lo, hi = _unpack_halves(_load_row_tiles(xs_ref, FFN_TM))
        lo = lo.astype(BF16)
        hi = hi.astype(BF16)
        hg = _dot(lo, wg16_ref[:HALF, :]) + _dot(hi, wg16_ref[HALF:, :])
        hu = _dot(lo, wu16_ref[:HALF, :]) + _dot(hi, wu16_ref[HALF:, :])
        y = _dot((_silu(hg) * hu).astype(BF16), wd16_ref[...])
        _store_row_tiles(ys_ref, _pack_halves(y[:, :HALF], y[:, HALF:]))


def moe_experts(xs, tile_expert, n_used, w_gate, w_up, w_down):
    n_tiles = xs.shape[0] // (FFN_TM * ROW_SUB)
    wspec = lambda shp: pl.BlockSpec((1,) + shp, lambda i, te, nu: (te[i], 0, 0))
    rows = pl.BlockSpec((FFN_TM * ROW_SUB, ROW_LANES), lambda i, te, nu: (jnp.minimum(i, nu[0] - 1), 0))
    grid_spec = pltpu.PrefetchScalarGridSpec(
        num_scalar_prefetch=2,
        grid=(n_tiles,),
        in_specs=[rows, wspec((D_MODEL, EXPERT_DIM)), wspec((D_MODEL, EXPERT_DIM)),
                  wspec((EXPERT_DIM, D_MODEL))],
        out_specs=rows,
        scratch_shapes=[pltpu.VMEM((D_MODEL, EXPERT_DIM), BF16), pltpu.VMEM((D_MODEL, EXPERT_DIM), BF16),
                        pltpu.VMEM((EXPERT_DIM, D_MODEL), BF16)],
    )
    return pl.pallas_call(
        _ffn_kernel,
        grid_spec=grid_spec,
        out_shape=jax.ShapeDtypeStruct(xs.shape, xs.dtype),
        compiler_params=_cparams("arbitrary"),
        name="moe_experts",
    )(tile_expert, n_used, xs, w_gate, w_up, w_down)


def _combine_kernel(pos_ref, gate_ref, h32_ref, h16_ref, sg_ref, su_ref, sd_ref, lng_ref, lnb_ref, ys_hbm,
                    o32_ref, o16_ref, buf_ref, sem, *, tb):
    def copy(p, j, t):
        return pltpu.make_async_copy(ys_hbm.at[_frame_rows(p), :], buf_ref.at[j, _frame_rows(t), :], sem)

    def issue(t, c):
        for j in range(TOP_K):
            copy(pos_ref[j, t], j, t).start()
        return c

    lax.fori_loop(0, tb, issue, 0)

    x = h16_ref[...]
    hdn = (_silu(_dot(x, sg_ref[...])) * _dot(x, su_ref[...])).astype(BF16)
    shared = _dot(hdn, sd_ref[...])
    acc_lo = shared[:, :HALF] + DN_ALPHA * h32_ref[:, :HALF]
    acc_hi = shared[:, HALF:] + DN_ALPHA * h32_ref[:, HALF:]

    def drain(t, c):
        for j in range(TOP_K):
            copy(0, 0, 0).wait()
        return c

    lax.fori_loop(0, tb, drain, 0)

    for j in range(TOP_K):
        lo, hi = _unpack_halves(_load_row_tiles(buf_ref.at[j], tb))
        gj = gate_ref[:, j:j + 1]
        acc_lo = acc_lo + gj * lo
        acc_hi = acc_hi + gj * hi
    mu = (jnp.sum(acc_lo, axis=-1, keepdims=True) + jnp.sum(acc_hi, axis=-1, keepdims=True)) / D_MODEL
    c_lo = acc_lo - mu
    c_hi = acc_hi - mu
    var = (jnp.sum(c_lo * c_lo, axis=-1, keepdims=True) + jnp.sum(c_hi * c_hi, axis=-1, keepdims=True)) / D_MODEL
    inv = lax.rsqrt(var + NORM_EPS)
    o_lo = c_lo * inv * lng_ref[:, :HALF] + lnb_ref[:, :HALF]
    o_hi = c_hi * inv * lng_ref[:, HALF:] + lnb_ref[:, HALF:]
    o32_ref[:, :HALF] = o_lo
    o32_ref[:, HALF:] = o_hi
    o16_ref[:, :HALF] = o_lo.astype(BF16)
    o16_ref[:, HALF:] = o_hi.astype(BF16)


def moe_combine(ys, pos8, gate_t, h32, h16, s_gate, s_up, s_down, ln_g, ln_b, tb=128):
    m, d = h32.shape
    tb = min(tb, m)
    row = lambda w: pl.BlockSpec((tb, w), lambda i: (i, 0))
    return pl.pallas_call(
        functools.partial(_combine_kernel, tb=tb),
        grid=(m // tb,),
        in_specs=[pl.BlockSpec((TOP_K, tb), lambda i: (0, i), memory_space=pltpu.SMEM),
                  row(TOP_K), row(d), row(d),
                  _const_spec(s_gate.shape), _const_spec(s_up.shape), _const_spec(s_down.shape),
                  _const_spec((1, d)), _const_spec((1, d)),
                  pl.BlockSpec(memory_space=pl.ANY)],
        out_specs=[row(d), row(d)],
        out_shape=[jax.ShapeDtypeStruct((m, d), F32), jax.ShapeDtypeStruct((m, d), BF16)],
        scratch_shapes=[pltpu.VMEM((TOP_K, tb * ROW_SUB, ROW_LANES), jnp.uint32), pltpu.SemaphoreType.DMA(())],
        compiler_params=_cparams("arbitrary"),
        name="moe_combine",
    )(pos8, gate_t, h32, h16, s_gate, s_up, s_down, ln_g.reshape(1, d), ln_b.reshape(1, d), ys)


def moe_layer(h32, h16, hpk, router, bias, w_gate, w_up, w_down, s_gate, s_up, s_down, ln_g, ln_b):
    m = h32.shape[0]
    eid8, rank8, gate8, counts = moe_route(h32, router, bias)
    cnt = counts[:, 0].astype(jnp.int32)
    seg = ((cnt + FFN_TM - 1) // FFN_TM) * FFN_TM
    ends = jnp.cumsum(seg)
    offs = ends - seg
    n_rows = m * TOP_K + N_EXPERTS * FFN_TM
    n_tiles = n_rows // FFN_TM
    experts = jnp.arange(N_EXPERTS, dtype=jnp.int32)
    pos8 = jnp.sum(jnp.where(eid8[:, :, None] == experts, offs, 0), axis=-1) + rank8
    tile_start = jnp.arange(n_tiles, dtype=jnp.int32) * FFN_TM
    tile_expert = jnp.minimum(jnp.sum((ends[None, :] <= tile_start[:, None]).astype(jnp.int32), axis=1),
                              N_EXPERTS - 1)
    n_used = (ends[-1:] // FFN_TM).astype(jnp.int32)
    xs = moe_dispatch(hpk, pos8, seg, ends, n_rows)
    ys = moe_experts(xs, tile_expert, n_used, w_gate, w_up, w_down)
    return moe_combine(ys, pos8, gate8.T, h32, h16, s_gate.astype(BF16), s_up.astype(BF16),
                       s_down.astype(BF16), ln_g, ln_b)


def _even_mixer(h16, lb, in_proj, out_proj, hg_gn_w, rw_mix, rw_w0, rw_w2, rw_a0, rw_a2, rw_g2, rw_kk, rw_ka,
                rw_rk, rw_ln_w, rw_ln_b, batch, seq):
    hg_cols = 4 * HG_W
    main_end = hg_cols + 3 * RW_W
    c1 = main_end + RW_DECAY_RANK
    c2 = c1 + RW_A_RANK
    padc = lambda w, n: jnp.pad(w, ((0, 0), (0, n - w.shape[1])))
    w_low = jnp.concatenate([padc(in_proj[:, main_end:c1], 128), padc(in_proj[:, c1:c2], 128),
                             padc(in_proj[:, c2:], 256)], axis=1)
    p_hg = matmul(h16, in_proj[:, :hg_cols].astype(BF16), F32)
    p_main = matmul(h16, in_proj[:, hg_cols:main_end].astype(BF16), F32)
    p_low = matmul(h16, w_low.astype(BF16), F32)
    o_a = hgrn2(p_hg, lb, hg_gn_w, batch, seq)
    o_b, gate_b = rwkv7(p_main, p_low, rw_mix, rw_w0, rw_w2, rw_a0, rw_a2, rw_g2, rw_kk, rw_ka, rw_rk, rw_ln_w,
                        rw_ln_b, batch, seq)
    return o_a, o_b, gate_b


def _odd_mixer(h16, in_proj, conv_w, conv_b, dt_bias, a_log, d_skip, norm_w, batch, seq):
    c1 = SSD_INNER
    c2 = c1 + SSD_CONV_DIM
    z = matmul(h16, in_proj[:, :c1].astype(BF16), BF16)
    xbc = matmul(h16, in_proj[:, c1:c2].astype(BF16), F32)
    dt = matmul(h16, jnp.pad(in_proj[:, c2:], ((0, 0), (0, 128 - SSD_HEADS))).astype(BF16), F32)
    return (ssd(z, xbc, dt, conv_w, conv_b, dt_bias, a_log, d_skip, norm_w, batch, seq),)


def kernel(x, hg_lb, ev_in_proj, ev_out_proj, hg_gn_w, rw_mix, rw_w0, rw_w2, rw_a0, rw_a2, rw_g2, rw_kk, rw_ka,
           rw_rk, rw_ln_w, rw_ln_b, od_in_proj, od_conv_w, od_conv_b, od_dt_bias, od_a_log, od_d, od_norm_w,
           od_out_proj, moe_router, moe_bias, moe_w_gate, moe_w_up, moe_w_down, sh_w_gate, sh_w_up, sh_w_down,
           ln1_g, ln1_b, ln2_g, ln2_b):
    batch, seq, d = x.shape
    lbs = jnp.cumsum(jax.nn.softmax(hg_lb.astype(F32), axis=0), axis=0)
    h32 = x.reshape(batch * seq, d)
    h16 = h32.astype(BF16)
    for l in range(DEPTH):
        if l % 2 == 0:
            e = l // 2
            mix = _even_mixer(h16, lbs[l], ev_in_proj[e], ev_out_proj[e], hg_gn_w[e], rw_mix[e], rw_w0[e],
                              rw_w2[e], rw_a0[e], rw_a2[e], rw_g2[e], rw_kk[e], rw_ka[e], rw_rk[e],
                              rw_ln_w[e], rw_ln_b[e], batch, seq)
            w_out = ev_out_proj[e]
        else:
            o = l // 2
            mix = _odd_mixer(h16, od_in_proj[o], od_conv_w[o], od_conv_b[o], od_dt_bias[o], od_a_log[o],
                             od_d[o], od_norm_w[o], batch, seq)
            w_out = od_out_proj[o]
        h32, h16, hpk = matmul_residual_ln(mix, w_out.astype(BF16), h32, ln1_g[l], ln1_b[l])
        h32, h16 = moe_layer(h32, h16, hpk, moe_router[l], moe_bias[l], moe_w_gate[l], moe_w_up[l],
                             moe_w_down[l], sh_w_gate[l], sh_w_up[l], sh_w_down[l], ln2_g[l], ln2_b[l])
    return h32.reshape(batch, seq, d)
```

```python
import functools
import math

import jax
import jax.numpy as jnp
import numpy as np
from jax import lax
from jax.experimental import pallas as pl
from jax.experimental.pallas import tpu as pltpu

F32 = jnp.float32
BF16 = jnp.bfloat16

D_MODEL = 2048
DEPTH = 2
CHUNK = 64

HG_HEADS = 8
HG_D = 128
HG_W = HG_HEADS * HG_D
RW_HEADS = 16
RW_N = 64
RW_W = RW_HEADS * RW_N
RW_DECAY_RANK = 64
RW_A_RANK = 64
RW_GATE_RANK = 160
RW_GN_EPS = 64e-5
RW_LOW_PAD = 512

SSD_INNER = 2 * D_MODEL
SSD_P = 64
SSD_HEADS = SSD_INNER // SSD_P
SSD_GROUPS = 8
SSD_HPG = SSD_HEADS // SSD_GROUPS
SSD_N = 128
SSD_CONV = 4
SSD_BC = SSD_GROUPS * SSD_N
SSD_CONV_DIM = SSD_INNER + 2 * SSD_BC
SSD_GW = SSD_HPG * SSD_P

N_EXPERTS = 64
TOP_K = 8
N_GROUPS = 8
TOPK_GROUPS = 4
EXPERT_DIM = 512
ROUTED_SCALE = 2.5

NORM_EPS = 1e-5
DN_ALPHA = (2 * DEPTH) ** 0.25

VMEM_LIMIT = 56 * 1024 * 1024


def _cparams(*sem):
    return pltpu.CompilerParams(dimension_semantics=sem, vmem_limit_bytes=VMEM_LIMIT)


def _const_spec(shape):
    nd = len(shape)
    return pl.BlockSpec(shape, lambda *_: (0,) * nd, pipeline_mode=pl.Buffered(1))


def _split3(x):
    hi = x.astype(BF16)
    r1 = x - hi.astype(F32)
    mid = r1.astype(BF16)
    lo = (r1 - mid.astype(F32)).astype(BF16)
    return hi, mid, lo


def _dot(a, b):
    return jnp.dot(a, b, preferred_element_type=F32)


def _dot_exact_lhs(m_bf16, x):
    hi, mid, lo = _split3(x)
    return _dot(m_bf16, hi) + _dot(m_bf16, mid) + _dot(m_bf16, lo)


def _dot_exact_rhs(x, m_bf16):
    hi, mid, lo = _split3(x)
    return _dot(hi, m_bf16) + _dot(mid, m_bf16) + _dot(lo, m_bf16)


def _sigmoid(x):
    return 1.0 / (1.0 + jnp.exp(-x))


def _silu(x):
    return x * _sigmoid(x)


def _softplus(x):
    return jnp.maximum(x, 0.0) + jnp.log(1.0 + jnp.exp(-jnp.abs(x)))


def _mm_kernel(x_ref, w_ref, o_ref):
    o_ref[...] = _dot(x_ref[...], w_ref[...]).astype(o_ref.dtype)


def matmul(x, w, out_dtype, tm=1024, tn=512):
    m, k = x.shape
    n = w.shape[1]
    tm = min(tm, m)
    tn = min(tn, n)
    assert m % tm == 0 and n % tn == 0
    return pl.pallas_call(
        _mm_kernel,
        grid=(m // tm, n // tn),
        in_specs=[pl.BlockSpec((tm, k), lambda i, j: (i, 0)),
                  pl.BlockSpec((k, tn), lambda i, j: (0, j))],
        out_specs=pl.BlockSpec((tm, tn), lambda i, j: (i, j)),
        out_shape=jax.ShapeDtypeStruct((m, n), out_dtype),
        compiler_params=_cparams("parallel", "arbitrary"),
        name="matmul",
    )(x, w)


def _layer_norm_rows(y, g, b):
    mu = jnp.mean(y, axis=-1, keepdims=True)
    yc = y - mu
    var = jnp.mean(yc * yc, axis=-1, keepdims=True)
    return yc * lax.rsqrt(var + NORM_EPS) * g + b


def _bf16_bits(x):
    u = lax.bitcast_convert_type(x, jnp.uint32)
    u = u + jnp.uint32(0x7FFF) + ((u >> 16) & jnp.uint32(1))
    return u & jnp.uint32(0xFFFF0000)


def _pack_halves(lo, hi):
    return (_bf16_bits(lo) >> 16) | _bf16_bits(hi)


def _unpack_halves(u):
    lo = lax.bitcast_convert_type(u << 16, F32)
    hi = lax.bitcast_convert_type(u & jnp.uint32(0xFFFF0000), F32)
    return lo, hi


HALF = D_MODEL // 2
ROW_SUB = 8
ROW_LANES = HALF // ROW_SUB


def _store_row_tiles(ref, packed):
    n = packed.shape[0]
    for c in range(ROW_SUB):
        ref[pl.ds(c, n, stride=ROW_SUB), :] = packed[:, c * ROW_LANES:(c + 1) * ROW_LANES]


def _load_row_tiles(ref, n):
    return jnp.concatenate([ref[pl.ds(c, n, stride=ROW_SUB), :] for c in range(ROW_SUB)], axis=1)


def _ln_outputs(y, g_ref, b_ref, o32_ref, o16_ref, opk_ref):
    o = _layer_norm_rows(y, g_ref[...], b_ref[...])
    o32_ref[...] = o
    o16_ref[...] = o.astype(BF16)
    _store_row_tiles(opk_ref, _pack_halves(o[:, :HALF], o[:, HALF:]))


def _mm_ln_kernel(x_ref, w_ref, h_ref, g_ref, b_ref, o32_ref, o16_ref, opk_ref):
    y = _dot(x_ref[...], w_ref[...]) + DN_ALPHA * h_ref[...]
    _ln_outputs(y, g_ref, b_ref, o32_ref, o16_ref, opk_ref)


def _mm2_ln_kernel(xa_ref, xb_ref, gate_ref, w_ref, h_ref, g_ref, b_ref, o32_ref, o16_ref, opk_ref):
    ka = xa_ref.shape[1]
    xb = (xb_ref[...] * gate_ref[...]).astype(BF16)
    y = _dot(xa_ref[...], w_ref[:ka, :]) + _dot(xb, w_ref[ka:, :]) + DN_ALPHA * h_ref[...]
    _ln_outputs(y, g_ref, b_ref, o32_ref, o16_ref, opk_ref)


def matmul_residual_ln(xs, w, h, g, b, tm=256):
    m = xs[0].shape[0]
    k, d = w.shape
    tm = min(tm, m)
    assert m % tm == 0
    row = lambda a: pl.BlockSpec((tm, a.shape[1]), lambda i: (i, 0))
    return pl.pallas_call(
        _mm_ln_kernel if len(xs) == 1 else _mm2_ln_kernel,
        grid=(m // tm,),
        in_specs=[row(a) for a in xs] + [_const_spec((k, d)), row(h), _const_spec((1, d)), _const_spec((1, d))],
        out_specs=[pl.BlockSpec((tm, d), lambda i: (i, 0)),
                   pl.BlockSpec((tm, d), lambda i: (i, 0)),
                   pl.BlockSpec((tm * ROW_SUB, ROW_LANES), lambda i: (i, 0))],
        out_shape=[jax.ShapeDtypeStruct((m, d), F32), jax.ShapeDtypeStruct((m, d), BF16),
                   jax.ShapeDtypeStruct((m * ROW_SUB, ROW_LANES), jnp.uint32)],
        compiler_params=_cparams("parallel"),
        name="matmul_residual_ln",
    )(*xs, w, h, g.reshape(1, d), b.reshape(1, d))


HG_LEVELS = (32, 16, 8, 4, 2, 1)


def _hgrn2_constants():
    t = np.arange(CHUNK)[:, None]
    u = np.arange(CHUNK)[None, :]
    blocks = [(u <= t)]
    for b in HG_LEVELS:
        start = (t // b) * b
        end = start + b - 1
        blocks.append((u >= start) & (u <= t))
        blocks.append((u > t) & (u <= end))
    blocks.append(u > t)
    cm = np.concatenate(blocks, axis=0).astype(np.float32)
    s = u
    masks = [((t // (2 * b)) == (s // (2 * b))) & ((t // b) % 2 == 1) & ((s // b) % 2 == 0) for b in HG_LEVELS]
    return cm, np.stack(masks).astype(np.float32)


HG_HPS = 4


def _hgrn2_kernel(q_ref, f_ref, i_ref, g_ref, lb_ref, gnw_ref, cm_ref, mask_ref, o_ref, st_ref, *, n_chunks):
    @pl.when(pl.program_id(2) == 0)
    def _():
        st_ref[...] = jnp.zeros_like(st_ref)

    lb = lb_ref[...]
    cm = cm_ref[...]
    nt = (((1,), (1,)), ((), ()))
    tail = CHUNK + 2 * CHUNK * len(HG_LEVELS)

    def chunk(c, carry):
        rows = pl.ds(pl.multiple_of(c * CHUNK, CHUNK), CHUNK)
        fp_all = f_ref[rows, :]
        lf_all = jnp.log(lb + (1.0 - lb) * _sigmoid(fp_all))
        k_all = (1.0 - lb) * _sigmoid(-fp_all)
        seg_all = _dot_exact_lhs(cm, lf_all)
        for hh in range(HG_HPS):
            cols = slice(hh * HG_D, (hh + 1) * HG_D)
            q = q_ref[rows, cols]
            v = i_ref[rows, cols]
            g = g_ref[rows, cols]
            k = k_all[:, cols]
            seg = seg_all[:, cols]
            bc = seg[0:CHUNK]
            st = st_ref[hh]
            o = lax.dot_general((q * jnp.exp(bc)).astype(BF16), st.astype(BF16), nt, preferred_element_type=F32)
            att = jnp.zeros((CHUNK, CHUNK), F32)
            for li in range(len(HG_LEVELS)):
                base = CHUNK + 2 * CHUNK * li
                qs = q * jnp.exp(seg[base:base + CHUNK])
                ks = k * jnp.exp(seg[base + CHUNK:base + 2 * CHUNK])
                a_l = lax.dot_general(qs.astype(BF16), ks.astype(BF16), nt, preferred_element_type=F32)
                att = att + a_l * mask_ref[li]
            o = o + _dot(att.astype(BF16), v.astype(BF16)) + jnp.sum(q * k, axis=-1, keepdims=True) * v
            kd = k * jnp.exp(seg[tail:tail + CHUNK])
            st_ref[hh] = st * jnp.exp(bc[CHUNK - 1:CHUNK]) + _dot(v.T.astype(BF16), kd.astype(BF16))
            ms = jnp.mean(o * o, axis=-1, keepdims=True)
            o_ref[rows, cols] = (o * lax.rsqrt(ms + NORM_EPS) * gnw_ref[:, cols] * _silu(g)).astype(o_ref.dtype)
        return carry

    lax.fori_loop(0, n_chunks, chunk, 0)


def hgrn2(p_hg, lb, gn_w, batch, seq, tb=256):
    tb = min(tb, seq)
    nt = seq // tb
    hg = HG_HEADS // HG_HPS
    w = HG_HPS * HG_D
    cm, masks = _hgrn2_constants()
    spec = lambda off: pl.BlockSpec((tb, w), lambda b, h, t: (b * nt + t, off * hg + h))
    return pl.pallas_call(
        functools.partial(_hgrn2_kernel, n_chunks=tb // CHUNK),
        grid=(batch, hg, nt),
        in_specs=[spec(0), spec(1), spec(2), spec(3),
                  pl.BlockSpec((1, w), lambda b, h, t: (0, h)),
                  pl.BlockSpec((1, w), lambda b, h, t: (0, h)),
                  _const_spec(cm.shape), _const_spec(masks.shape)],
        out_specs=pl.BlockSpec((tb, w), lambda b, h, t: (b * nt + t, h)),
        out_shape=jax.ShapeDtypeStruct((batch * seq, HG_W), BF16),
        scratch_shapes=[pltpu.VMEM((HG_HPS, HG_D, HG_D), F32)],
        compiler_params=_cparams("parallel", "parallel", "arbitrary"),
        name="hgrn2",
    )(p_hg, p_hg, p_hg, p_hg, lb.reshape(1, HG_W), gn_w.reshape(1, HG_W),
      jnp.asarray(cm, BF16), jnp.asarray(masks))


def _rwkv_prep_kernel(pm_ref, pmprev_ref, plo_ref, ploprev_ref, mixm_ref, mixl_ref, w0_ref, w2_ref, a0_ref,
                      a2_ref, g2_ref, r_o, k_o, v_o, w_o, a_o, g_o, *, tb, seq):
    first = (pl.program_id(0) * tb) % seq == 0
    keep = jnp.where(first, 0.0, 1.0)

    def shift_mix(cur, prev_blk, mix):
        row = lax.broadcasted_iota(jnp.int32, cur.shape, 0)
        shifted = jnp.where(row == 0, prev_blk[7:8, :] * keep, pltpu.roll(cur, 1, 0))
        return cur + (shifted - cur) * mix

    for idx, out in enumerate((r_o, k_o, v_o)):
        cols = slice(idx * RW_W, (idx + 1) * RW_W)
        out[...] = shift_mix(pm_ref[:, cols], pmprev_ref[:, cols], mixm_ref[:, cols])
    low = shift_mix(plo_ref[...], ploprev_ref[...], mixl_ref[...])
    wl = jnp.tanh(low[:, 0:128]).astype(BF16)
    al = low[:, 128:256].astype(BF16)
    gl = _sigmoid(low[:, 256:512]).astype(BF16)
    w_log = -_softplus(-(w0_ref[...] + _dot(wl, w2_ref[...]))) - 0.5
    w_o[...] = jnp.exp(-jnp.exp(w_log))
    a_o[...] = _sigmoid(a0_ref[...] + _dot(al, a2_ref[...]))
    g_o[...] = _dot(gl, g2_ref[...])


def rwkv_prep(p_main, p_low, mix_main, mix_low, w0, w2p, a0, a2p, g2p, seq, tb=256):
    m = p_main.shape[0]
    tb = min(tb, seq)
    prev = lambda i: (jnp.maximum(i * (tb // 8) - 1, 0), 0)
    out = jax.ShapeDtypeStruct((m, RW_W), F32)
    ospec = pl.BlockSpec((tb, RW_W), lambda i: (i, 0))
    return pl.pallas_call(
        functools.partial(_rwkv_prep_kernel, tb=tb, seq=seq),
        grid=(m // tb,),
        in_specs=[pl.BlockSpec((tb, 3 * RW_W), lambda i: (i, 0)),
                  pl.BlockSpec((8, 3 * RW_W), prev),
                  pl.BlockSpec((tb, RW_LOW_PAD), lambda i: (i, 0)),
                  pl.BlockSpec((8, RW_LOW_PAD), prev),
                  _const_spec((1, 3 * RW_W)), _const_spec((1, RW_LOW_PAD)),
                  _const_spec((1, RW_W)), _const_spec((128, RW_W)),
                  _const_spec((1, RW_W)), _const_spec((128, RW_W)),
                  _const_spec((256, RW_W))],
        out_specs=[ospec] * 6,
        out_shape=[out] * 6,
        compiler_params=_cparams("parallel"),
        name="rwkv_prep",
    )(p_main, p_main, p_low, p_low, mix_main, mix_low, w0, w2p, a0, a2p, g2p)


def _rwkv_scan_kernel(r_ref, k_ref, v_ref, w_ref, a_ref, kkp_ref, kap_ref, rkp_ref, lnw_ref, lnb_ref,
                      o_ref, s_ref, vec_ref, *, tt):
    @pl.when(pl.program_id(0) == 0)
    def _():
        s_ref[...] = jnp.zeros_like(s_ref)

    kkp = kkp_ref[...]
    kap = kap_ref[...]
    rkp = rkp_ref[...]
    lnw = lnw_ref[...]
    lnb = lnb_ref[...]

    def step(t, carry):
        kt = k_ref[t]
        at = a_ref[t]
        rt = r_ref[t]
        vt = v_ref[t]
        kk = kt * kkp
        nrm = jnp.sqrt(jnp.sum(kk * kk, axis=0, keepdims=True))
        kk = kk / jnp.maximum(nrm, 1e-12)
        kh = kt * (1.0 + (at - 1.0) * kap)
        vec_ref[0] = -kk
        vec_ref[1] = kk * at
        vec_ref[2] = kh
        sa = jnp.zeros((RW_N, kt.shape[1]), F32)
        for j in range(RW_N):
            sa = sa + s_ref[j] * vec_ref[0, j:j + 1, :]
        y = jnp.zeros_like(sa)
        for j in range(RW_N):
            sj = (s_ref[j] * w_ref[t, j:j + 1, :] + sa * vec_ref[1, j:j + 1, :]
                  + vt * vec_ref[2, j:j + 1, :])
            s_ref[j] = sj
            y = y + sj * r_ref[t, j:j + 1, :]
        mu = jnp.mean(y, axis=0, keepdims=True)
        yc = y - mu
        var = jnp.mean(yc * yc, axis=0, keepdims=True)
        yn = yc * lax.rsqrt(var + RW_GN_EPS) * lnw + lnb
        bonus = jnp.sum(rt * kh * rkp, axis=0, keepdims=True) * vt
        o_ref[t] = yn + bonus
        return carry

    lax.fori_loop(0, tt, step, 0, unroll=2)


def rwkv_scan(r, k, v, w, a, kkp, kap, rkp, lnw, lnb, tt=16):
    t, n, lanes = r.shape
    tt = min(tt, t)
    blk = pl.BlockSpec((tt, n, lanes), lambda i: (i, 0, 0))
    par = _const_spec((n, lanes))
    return pl.pallas_call(
        functools.partial(_rwkv_scan_kernel, tt=tt),
        grid=(t // tt,),
        in_specs=[blk] * 5 + [par] * 5,
        out_specs=blk,
        out_shape=jax.ShapeDtypeStruct((t, n, lanes), F32),
        scratch_shapes=[pltpu.VMEM((n, n, lanes), F32), pltpu.VMEM((3, n, lanes), F32)],
        compiler_params=_cparams("arbitrary"),
        name="rwkv_scan",
    )(r, k, v, w, a, kkp, kap, rkp, lnw, lnb)


def _time_major(x, batch, seq):
    return x.reshape(batch, seq, RW_HEADS, RW_N).transpose(1, 3, 0, 2).reshape(seq, RW_N, batch * RW_HEADS)


def _token_major(x, batch, seq):
    return x.reshape(seq, RW_N, batch, RW_HEADS).transpose(2, 0, 3, 1).reshape(batch * seq, RW_W)


def _head_param(p, batch):
    return jnp.tile(p.reshape(RW_HEADS, RW_N).T, (1, batch))


def _pad_rows(w, rows):
    return jnp.pad(w, ((0, rows - w.shape[0]), (0, 0)))


def rwkv7(p_main, p_low, mix, w0, w2, a0, a2, g2, k_k, k_a, r_k, ln_w, ln_b, batch, seq):
    mix_main = mix[:3 * RW_W].reshape(1, -1)
    lo = mix[3 * RW_W:]
    c1 = RW_DECAY_RANK
    c2 = c1 + RW_A_RANK
    mix_low = jnp.concatenate([jnp.pad(lo[:c1], (0, 128 - RW_DECAY_RANK)),
                               jnp.pad(lo[c1:c2], (0, 128 - RW_A_RANK)),
                               jnp.pad(lo[c2:], (0, 256 - RW_GATE_RANK))]).reshape(1, RW_LOW_PAD)
    r, k, v, w, a, g = rwkv_prep(
        p_main, p_low, mix_main, mix_low, w0.reshape(1, -1), _pad_rows(w2, 128).astype(BF16),
        a0.reshape(1, -1), _pad_rows(a2, 128).astype(BF16), _pad_rows(g2, 256).astype(BF16), seq)
    tm = lambda x: _time_major(x, batch, seq)
    hp = lambda p: _head_param(p.reshape(-1), batch)
    o = rwkv_scan(tm(r), tm(k), tm(v), tm(w), tm(a), hp(k_k), hp(k_a), hp(r_k), hp(ln_w), hp(ln_b))
    return _token_major(o, batch, seq), g


def _ssd_kernel(z_ref, xbc_ref, dt_ref, cw_ref, cb_ref, dtb_ref, alog_ref, dexp_ref, nw_ref, ex_ref, tri_ref,
                o_ref, xpad_ref, xact_ref, st_ref, y_ref):
    c = pl.program_id(1)

    @pl.when(c == 0)
    def _():
        st_ref[...] = jnp.zeros_like(st_ref)
        xpad_ref[0:8, :] = jnp.zeros((8, SSD_CONV_DIM), F32)

    xpad_ref[8:8 + CHUNK, :] = xbc_ref[...]
    ct = 512
    for j in range(SSD_CONV_DIM // ct):
        cols = slice(j * ct, (j + 1) * ct)
        win = xpad_ref[:, cols]
        acc = cw_ref[0:1, cols] * win
        for tap in range(1, SSD_CONV):
            acc = pltpu.roll(acc, 1, 0) + cw_ref[tap:tap + 1, cols] * win
        xact_ref[:, cols] = _silu(acc[8:] + cb_ref[:, cols])
    xpad_ref[0:8, :] = xpad_ref[CHUNK:CHUNK + 8, :]

    ex = ex_ref[...]
    dtv = _softplus(dt_ref[...] + dtb_ref[...])
    da = dtv * (-jnp.exp(alog_ref[...]))
    acum = _dot_exact_lhs(tri_ref[...], da)
    acum_t = acum.T
    dtx = _dot_exact_rhs(dtv, ex)
    acx = _dot_exact_rhs(acum, ex)
    alx = acx[CHUNK - 1:CHUNK, :]
    e_in = jnp.exp(acx)
    e_out = jnp.exp(alx - acx)
    e_last = jnp.exp(alx)
    xs = xact_ref[:, 0:SSD_INNER]
    xdt = xs * dtx
    ti = lax.broadcasted_iota(jnp.int32, (CHUNK, CHUNK), 0)
    si = lax.broadcasted_iota(jnp.int32, (CHUNK, CHUNK), 1)
    causal = ti >= si
    lane = lax.broadcasted_iota(jnp.int32, (CHUNK, 128), 1)
    nt = (((1,), (1,)), ((), ()))

    for g in range(SSD_GROUPS):
        bg = xact_ref[:, SSD_INNER + g * SSD_N:SSD_INNER + (g + 1) * SSD_N]
        cg = xact_ref[:, SSD_INNER + SSD_BC + g * SSD_N:SSD_INNER + SSD_BC + (g + 1) * SSD_N]
        cgb = cg.astype(BF16)
        cb = lax.dot_general(cgb, bg.astype(BF16), nt, preferred_element_type=F32)
        gc = slice(g * SSD_GW, (g + 1) * SSD_GW)
        sgt = st_ref[g]
        y_g = _dot(cgb, sgt.astype(BF16)) * e_in[:, gc]
        xdt_g = xdt[:, gc]
        for m in range(SSD_HPG // 2):
            xp = xdt_g[:, m * 128:(m + 1) * 128]
            yp = y_g[:, m * 128:(m + 1) * 128]
            for half in range(2):
                h = g * SSD_HPG + 2 * m + half
                seg = acum[:, h:h + 1] - acum_t[h:h + 1, :]
                lmat = jnp.where(causal, jnp.exp(jnp.minimum(seg, 0.0)), 0.0)
                rhs = jnp.where((lane >= 64) == (half == 1), xp, 0.0)
                yp = yp + _dot((cb * lmat).astype(BF16), rhs.astype(BF16))
            y_ref[:, g * SSD_GW + m * 128:g * SSD_GW + (m + 1) * 128] = yp
        st_ref[g] = sgt * e_last[:, gc] + _dot(bg.T.astype(BF16), (xdt_g * e_out[:, gc]).astype(BF16))

    for g in range(SSD_GROUPS):
        gc = slice(g * SSD_GW, (g + 1) * SSD_GW)
        y = y_ref[:, gc] + dexp_ref[:, gc] * xact_ref[:, gc]
        y = y * _silu(z_ref[:, gc].astype(F32))
        ms = jnp.mean(y * y, axis=-1, keepdims=True)
        o_ref[:, gc] = (y * lax.rsqrt(ms + NORM_EPS) * nw_ref[:, gc]).astype(o_ref.dtype)


def ssd(z, xbc, dt, conv_w, conv_b, dt_bias, a_log, d_skip, norm_w, batch, seq):
    nc = seq // CHUNK
    pad = lambda v: jnp.pad(v.reshape(1, -1), ((0, 0), (0, 128 - SSD_HEADS)))
    ex = np.zeros((128, SSD_INNER), np.float32)
    for h in range(SSD_HEADS):
        ex[h, h * SSD_P:(h + 1) * SSD_P] = 1.0
    tri = np.tril(np.ones((CHUNK, CHUNK), np.float32))
    row = lambda w: pl.BlockSpec((CHUNK, w), lambda b, c: (b * nc + c, 0))
    return pl.pallas_call(
        _ssd_kernel,
        grid=(batch, nc),
        in_specs=[row(SSD_INNER), row(SSD_CONV_DIM), row(128),
                  _const_spec((SSD_CONV, SSD_CONV_DIM)), _const_spec((1, SSD_CONV_DIM)),
                  _const_spec((1, 128)), _const_spec((1, 128)),
                  _const_spec((1, SSD_INNER)), _const_spec((1, SSD_INNER)),
                  _const_spec((128, SSD_INNER)), _const_spec((CHUNK, CHUNK))],
        out_specs=row(SSD_INNER),
        out_shape=jax.ShapeDtypeStruct((batch * seq, SSD_INNER), BF16),
        scratch_shapes=[pltpu.VMEM((CHUNK + 8, SSD_CONV_DIM), F32),
                        pltpu.VMEM((CHUNK, SSD_CONV_DIM), F32),
                        pltpu.VMEM((SSD_GROUPS, SSD_N, SSD_GW), F32),
                        pltpu.VMEM((CHUNK, SSD_INNER), F32)],
        compiler_params=_cparams("parallel", "arbitrary"),
        name="ssd",
    )(z, xbc, dt, conv_w, conv_b.reshape(1, -1), pad(dt_bias), pad(a_log),
      jnp.repeat(d_skip, SSD_P).reshape(1, -1), norm_w.reshape(1, -1), jnp.asarray(ex, BF16),
      jnp.asarray(tri, BF16))


FFN_TM = 256


def _router_kernel(x_ref, rt_ref, bias_ref, up_ref, ls_ref, eid_o, rank_o, gate_o, cnt_o, carry_ref, *, tb):
    @pl.when(pl.program_id(0) == 0)
    def _():
        carry_ref[...] = jnp.zeros_like(carry_ref)

    nt = (((1,), (1,)), ((), ()))
    x = x_ref[...]
    rt = rt_ref[...]
    xh = x.astype(BF16)
    xl = (x - xh.astype(F32)).astype(BF16)
    rh = rt.astype(BF16)
    rl = (rt - rh.astype(F32)).astype(BF16)
    dg = lambda a, b: lax.dot_general(a, b, nt, preferred_element_type=F32)
    logits = dg(rh, xh) + dg(rh, xl) + dg(rl, xh)
    scores = _sigmoid(logits)
    reps = tb // 128
    wide = lambda a: jnp.concatenate([a] * reps, axis=1) if reps > 1 else a
    biased = scores + wide(bias_ref[...])
    neg = -jnp.inf

    io8 = lax.broadcasted_iota(jnp.int32, (8, tb), 0)
    blocks, gs = [], []
    for g in range(N_GROUPS):
        blk = biased[8 * g:8 * g + 8, :]
        m1 = jnp.max(blk, axis=0, keepdims=True)
        first = jnp.min(jnp.where(blk == m1, io8, 8), axis=0, keepdims=True)
        m2 = jnp.max(jnp.where(io8 == first, neg, blk), axis=0, keepdims=True)
        blocks.append(blk)
        gs.append(m1 + m2)
    masked = []
    for g in range(N_GROUPS):
        ahead = jnp.zeros((1, tb), jnp.int32)
        for o in range(N_GROUPS):
            if o == g:
                continue
            beats = (gs[o] > gs[g]) | ((gs[o] == gs[g]) & (o < g))
            ahead = ahead + jnp.where(beats, 1, 0)
        masked.append(jnp.where(ahead < TOPK_GROUPS, blocks[g], neg))
    masked = jnp.concatenate(masked, axis=0)
    eidx = lax.broadcasted_iota(jnp.int32, (N_EXPERTS, tb), 0)
    ahead = jnp.zeros((N_EXPERTS, tb), jnp.int32)
    for o in range(N_EXPERTS):
        row = masked[o:o + 1, :]
        beats = (row > masked) | ((row == masked) & (eidx > o))
        ahead = ahead + jnp.where(beats, 1, 0)
    sel = ahead < TOP_K
    self_ = jnp.where(sel, 1.0, 0.0)
    wts = jnp.where(sel, scores, 0.0)
    gate = wts / jnp.sum(wts, axis=0, keepdims=True) * ROUTED_SCALE

    selb = self_.astype(BF16)
    carry = carry_ref[...]
    rank = _dot(selb, up_ref[...]) + wide(carry)
    new_carry = carry + _dot(selb, jnp.ones((tb, 128), BF16))
    carry_ref[...] = new_carry
    cnt_o[...] = new_carry
    before = _dot(ls_ref[...], selb)
    eidf = eidx.astype(F32)
    for j in range(TOP_K):
        hit = sel & (before == float(j))
        pick = lambda a: jnp.sum(jnp.where(hit, a, 0.0), axis=0, keepdims=True)
        eid_o[j:j + 1, :] = pick(eidf).astype(jnp.int32)
        rank_o[j:j + 1, :] = pick(rank).astype(jnp.int32)
        gate_o[j:j + 1, :] = pick(gate)


def moe_route(h32, router, bias, tb=256):
    m = h32.shape[0]
    tb = min(tb, m)
    up = np.triu(np.ones((tb, tb), np.float32), k=1)
    ls = np.tril(np.ones((N_EXPERTS, N_EXPERTS), np.float32), k=-1)
    slot = lambda dt: jax.ShapeDtypeStruct((TOP_K, m), dt)
    return pl.pallas_call(
        functools.partial(_router_kernel, tb=tb),
        grid=(m // tb,),
        in_specs=[pl.BlockSpec((tb, D_MODEL), lambda i: (i, 0)),
                  _const_spec((N_EXPERTS, D_MODEL)), _const_spec((N_EXPERTS, 128)),
                  _const_spec((tb, tb)), _const_spec((N_EXPERTS, N_EXPERTS))],
        out_specs=[pl.BlockSpec((TOP_K, tb), lambda i: (0, i))] * 3
                  + [pl.BlockSpec((N_EXPERTS, 128), lambda i: (0, 0))],
        out_shape=[slot(jnp.int32), slot(jnp.int32), slot(F32),
                   jax.ShapeDtypeStruct((N_EXPERTS, 128), F32)],
        scratch_shapes=[pltpu.VMEM((N_EXPERTS, 128), F32)],
        compiler_params=_cparams("arbitrary"),
        name="moe_route",
    )(h32, router.T, jnp.broadcast_to(bias.reshape(N_EXPERTS, 1), (N_EXPERTS, 128)),
      jnp.asarray(up, BF16), jnp.asarray(ls, BF16))


def _frame_rows(p):
    return pl.ds(pl.multiple_of(p * ROW_SUB, ROW_SUB), ROW_SUB)


def _dispatch_kernel(seg_ref, ends_ref, pos_ref, x_ref, xs_out, zero_ref, sem, zsem, *, tb):
    tile_rows = FFN_TM * ROW_SUB

    @pl.when(pl.program_id(0) == 0)
    def _():
        zero_ref[...] = jnp.zeros_like(zero_ref)

        def fill(e):
            start = pl.multiple_of((ends_ref[e] - FFN_TM) * ROW_SUB, tile_rows)
            return pltpu.make_async_copy(zero_ref, xs_out.at[pl.ds(start, tile_rows), :], zsem)

        def start_fill(e, c):
            @pl.when(seg_ref[e] > 0)
            def _():
                fill(e).start()
            return c

        def wait_fill(e, c):
            @pl.when(seg_ref[e] > 0)
            def _():
                fill(e).wait()
            return c

        lax.fori_loop(0, N_EXPERTS, start_fill, 0)
        lax.fori_loop(0, N_EXPERTS, wait_fill, 0)

    def copy(t, p):
        return pltpu.make_async_copy(x_ref.at[_frame_rows(t), :], xs_out.at[_frame_rows(p), :], sem)

    def issue(t, c):
        for j in range(TOP_K):
            copy(t, pos_ref[j, t]).start()
        return c

    lax.fori_loop(0, tb, issue, 0)

    def drain(t, c):
        for j in range(TOP_K):
            copy(0, 0).wait()
        return c

    lax.fori_loop(0, tb, drain, 0)


def moe_dispatch(xpk, pos8, seg, ends, n_rows, tb=256):
    m = xpk.shape[0] // ROW_SUB
    tb = min(tb, m)
    grid_spec = pltpu.PrefetchScalarGridSpec(
        num_scalar_prefetch=2,
        grid=(m // tb,),
        in_specs=[pl.BlockSpec((TOP_K, tb), lambda i, sg, en: (0, i), memory_space=pltpu.SMEM),
                  pl.BlockSpec((tb * ROW_SUB, ROW_LANES), lambda i, sg, en: (i, 0))],
        out_specs=pl.BlockSpec(memory_space=pl.ANY),
        scratch_shapes=[pltpu.VMEM((FFN_TM * ROW_SUB, ROW_LANES), xpk.dtype),
                        pltpu.SemaphoreType.DMA(()), pltpu.SemaphoreType.DMA(())],
    )
    return pl.pallas_call(
        functools.partial(_dispatch_kernel, tb=tb),
        grid_spec=grid_spec,
        out_shape=jax.ShapeDtypeStruct((n_rows * ROW_SUB, ROW_LANES), xpk.dtype),
        compiler_params=_cparams("arbitrary"),
        name="moe_dispatch",
    )(seg, ends, pos8, xpk)


def _ffn_kernel(te_ref, nu_ref, xs_ref, wg_ref, wu_ref, wd_ref, ys_ref, wg16_ref, wu16_ref, wd16_ref):
    i = pl.program_id(0)

    @pl.when((i == 0) | (te_ref[i] != te_ref[jnp.maximum(i - 1, 0)]))
    def _():
        wg16_ref[...] = wg_ref[0].astype(BF16)
        wu16_ref[...] = wu_ref[0].astype(BF16)
        wd16_ref[...] = wd_ref[0].astype(BF16)

    @pl.when(i < nu_ref[0])
    def _():
        lo, hi = _unpack_halves(_load_row_tiles(xs_ref, FFN_TM))
        lo = lo.astype(BF16)
        hi = hi.astype(BF16)
        hg = _dot(lo, wg16_ref[:HALF, :]) + _dot(hi, wg16_ref[HALF:, :])
        hu = _dot(lo, wu16_ref[:HALF, :]) + _dot(hi, wu16_ref[HALF:, :])
        y = _dot((_silu(hg) * hu).astype(BF16), wd16_ref[...])
        _store_row_tiles(ys_ref, _pack_halves(y[:, :HALF], y[:, HALF:]))


def moe_experts(xs, tile_expert, n_used, w_gate, w_up, w_down, layer):
    n_tiles = xs.shape[0] // (FFN_TM * ROW_SUB)
    wspec = lambda shp: pl.BlockSpec((None, 1) + shp, lambda i, te, nu: (layer, te[i], 0, 0))
    rows = pl.BlockSpec((FFN_TM * ROW_SUB, ROW_LANES), lambda i, te, nu: (jnp.minimum(i, nu[0] - 1), 0))
    grid_spec = pltpu.PrefetchScalarGridSpec(
        num_scalar_prefetch=2,
        grid=(n_tiles,),
        in_specs=[rows, wspec((D_MODEL, EXPERT_DIM)), wspec((D_MODEL, EXPERT_DIM)),
                  wspec((EXPERT_DIM, D_MODEL))],
        out_specs=rows,
        scratch_shapes=[pltpu.VMEM((D_MODEL, EXPERT_DIM), BF16), pltpu.VMEM((D_MODEL, EXPERT_DIM), BF16),
                        pltpu.VMEM((EXPERT_DIM, D_MODEL), BF16)],
    )
    return pl.pallas_call(
        _ffn_kernel,
        grid_spec=grid_spec,
        out_shape=jax.ShapeDtypeStruct(xs.shape, xs.dtype),
        compiler_params=_cparams("arbitrary"),
        name="moe_experts",
    )(tile_expert, n_used, xs, w_gate, w_up, w_down)


def _combine_kernel(pos_ref, gate_ref, h32_ref, h16_ref, sg_ref, su_ref, sd_ref, lng_ref, lnb_ref, ys_hbm,
                    o32_ref, o16_ref, buf_ref, sem, *, tb):
    def copy(p, j, t):
        return pltpu.make_async_copy(ys_hbm.at[_frame_rows(p), :], buf_ref.at[j, _frame_rows(t), :], sem)

    def issue(t, c):
        for j in range(TOP_K):
            copy(pos_ref[j, t], j, t).start()
        return c

    lax.fori_loop(0, tb, issue, 0)

    x = h16_ref[...]
    hdn = (_silu(_dot(x, sg_ref[...])) * _dot(x, su_ref[...])).astype(BF16)
    shared = _dot(hdn, sd_ref[...])
    acc_lo = shared[:, :HALF] + DN_ALPHA * h32_ref[:, :HALF]
    acc_hi = shared[:, HALF:] + DN_ALPHA * h32_ref[:, HALF:]

    def drain(t, c):
        for j in range(TOP_K):
            copy(0, 0, 0).wait()
        return c

    lax.fori_loop(0, tb, drain, 0)

    for j in range(TOP_K):
        lo, hi = _unpack_halves(_load_row_tiles(buf_ref.at[j], tb))
        gj = gate_ref[:, j:j + 1]
        acc_lo = acc_lo + gj * lo
        acc_hi = acc_hi + gj * hi
    mu = (jnp.sum(acc_lo, axis=-1, keepdims=True) + jnp.sum(acc_hi, axis=-1, keepdims=True)) / D_MODEL
    c_lo = acc_lo - mu
    c_hi = acc_hi - mu
    var = (jnp.sum(c_lo * c_lo, axis=-1, keepdims=True) + jnp.sum(c_hi * c_hi, axis=-1, keepdims=True)) / D_MODEL
    inv = lax.rsqrt(var + NORM_EPS)
    o_lo = c_lo * inv * lng_ref[:, :HALF] + lnb_ref[:, :HALF]
    o_hi = c_hi * inv * lng_ref[:, HALF:] + lnb_ref[:, HALF:]
    o32_ref[:, :HALF] = o_lo
    o32_ref[:, HALF:] = o_hi
    o16_ref[:, :HALF] = o_lo.astype(BF16)
    o16_ref[:, HALF:] = o_hi.astype(BF16)


def moe_combine(ys, pos8, gate_t, h32, h16, s_gate, s_up, s_down, ln_g, ln_b, tb=128):
    m, d = h32.shape
    tb = min(tb, m)
    row = lambda w: pl.BlockSpec((tb, w), lambda i: (i, 0))
    return pl.pallas_call(
        functools.partial(_combine_kernel, tb=tb),
        grid=(m // tb,),
        in_specs=[pl.BlockSpec((TOP_K, tb), lambda i: (0, i), memory_space=pltpu.SMEM),
                  row(TOP_K), row(d), row(d),
                  _const_spec(s_gate.shape), _const_spec(s_up.shape), _const_spec(s_down.shape),
                  _const_spec((1, d)), _const_spec((1, d)),
                  pl.BlockSpec(memory_space=pl.ANY)],
        out_specs=[row(d), row(d)],
        out_shape=[jax.ShapeDtypeStruct((m, d), F32), jax.ShapeDtypeStruct((m, d), BF16)],
        scratch_shapes=[pltpu.VMEM((TOP_K, tb * ROW_SUB, ROW_LANES), jnp.uint32), pltpu.SemaphoreType.DMA(())],
        compiler_params=_cparams("arbitrary"),
        name="moe_combine",
    )(pos8, gate_t, h32, h16, s_gate, s_up, s_down, ln_g.reshape(1, d), ln_b.reshape(1, d), ys)


def moe_layer(h32, h16, hpk, router, bias, w_gate, w_up, w_down, layer, s_gate, s_up, s_down, ln_g, ln_b):
    m = h32.shape[0]
    eid8, rank8, gate8, counts = moe_route(h32, router, bias)
    cnt = counts[:, 0].astype(jnp.int32)
    seg = ((cnt + FFN_TM - 1) // FFN_TM) * FFN_TM
    ends = jnp.cumsum(seg)
    offs = ends - seg
    n_rows = m * TOP_K + N_EXPERTS * FFN_TM
    n_tiles = n_rows // FFN_TM
    experts = jnp.arange(N_EXPERTS, dtype=jnp.int32)
    pos8 = jnp.sum(jnp.where(eid8[:, :, None] == experts, offs, 0), axis=-1) + rank8
    tile_start = jnp.arange(n_tiles, dtype=jnp.int32) * FFN_TM
    tile_expert = jnp.minimum(jnp.sum((ends[None, :] <= tile_start[:, None]).astype(jnp.int32), axis=1),
                              N_EXPERTS - 1)
    n_used = (ends[-1:] // FFN_TM).astype(jnp.int32)
    xs = moe_dispatch(hpk, pos8, seg, ends, n_rows)
    ys = moe_experts(xs, tile_expert, n_used, w_gate, w_up, w_down, layer)
    return moe_combine(ys, pos8, gate8.T, h32, h16, s_gate.astype(BF16), s_up.astype(BF16),
                       s_down.astype(BF16), ln_g, ln_b)


def _even_mixer(h16, lb, in_proj, out_proj, hg_gn_w, rw_mix, rw_w0, rw_w2, rw_a0, rw_a2, rw_g2, rw_kk, rw_ka,
                rw_rk, rw_ln_w, rw_ln_b, batch, seq):
    hg_cols = 4 * HG_W
    main_end = hg_cols + 3 * RW_W
    c1 = main_end + RW_DECAY_RANK
    c2 = c1 + RW_A_RANK
    padc = lambda w, n: jnp.pad(w, ((0, 0), (0, n - w.shape[1])))
    w_low = jnp.concatenate([padc(in_proj[:, main_end:c1], 128), padc(in_proj[:, c1:c2], 128),
                             padc(in_proj[:, c2:], 256)], axis=1)
    p_hg = matmul(h16, in_proj[:, :hg_cols].astype(BF16), F32)
    p_main = matmul(h16, in_proj[:, hg_cols:main_end].astype(BF16), F32)
    p_low = matmul(h16, w_low.astype(BF16), F32)
    o_a = hgrn2(p_hg, lb, hg_gn_w, batch, seq)
    o_b, gate_b = rwkv7(p_main, p_low, rw_mix, rw_w0, rw_w2, rw_a0, rw_a2, rw_g2, rw_kk, rw_ka, rw_rk, rw_ln_w,
                        rw_ln_b, batch, seq)
    return o_a, o_b, gate_b


def _odd_mixer(h16, in_proj, conv_w, conv_b, dt_bias, a_log, d_skip, norm_w, batch, seq):
    c1 = SSD_INNER
    c2 = c1 + SSD_CONV_DIM
    z = matmul(h16, in_proj[:, :c1].astype(BF16), BF16)
    xbc = matmul(h16, in_proj[:, c1:c2].astype(BF16), F32)
    dt = matmul(h16, jnp.pad(in_proj[:, c2:], ((0, 0), (0, 128 - SSD_HEADS))).astype(BF16), F32)
    return (ssd(z, xbc, dt, conv_w, conv_b, dt_bias, a_log, d_skip, norm_w, batch, seq),)


def kernel(x, hg_lb, ev_in_proj, ev_out_proj, hg_gn_w, rw_mix, rw_w0, rw_w2, rw_a0, rw_a2, rw_g2, rw_kk, rw_ka,
           rw_rk, rw_ln_w, rw_ln_b, od_in_proj, od_conv_w, od_conv_b, od_dt_bias, od_a_log, od_d, od_norm_w,
           od_out_proj, moe_router, moe_bias, moe_w_gate, moe_w_up, moe_w_down, sh_w_gate, sh_w_up, sh_w_down,
           ln1_g, ln1_b, ln2_g, ln2_b):
    batch, seq, d = x.shape
    lbs = jnp.cumsum(jax.nn.softmax(hg_lb.astype(F32), axis=0), axis=0)
    h32 = x.reshape(batch * seq, d)
    h16 = h32.astype(BF16)
    for l in range(DEPTH):
        if l % 2 == 0:
            e = l // 2
            mix = _even_mixer(h16, lbs[l], ev_in_proj[e], ev_out_proj[e], hg_gn_w[e], rw_mix[e], rw_w0[e],
                              rw_w2[e], rw_a0[e], rw_a2[e], rw_g2[e], rw_kk[e], rw_ka[e], rw_rk[e],
                              rw_ln_w[e], rw_ln_b[e], batch, seq)
            w_out = ev_out_proj[e]
        else:
            o = l // 2
            mix = _odd_mixer(h16, od_in_proj[o], od_conv_w[o], od_conv_b[o], od_dt_bias[o], od_a_log[o],
                             od_d[o], od_norm_w[o], batch, seq)
            w_out = od_out_proj[o]
        h32, h16, hpk = matmul_residual_ln(mix, w_out.astype(BF16), h32, ln1_g[l], ln1_b[l])
        h32, h16 = moe_layer(h32, h16, hpk, moe_router[l], moe_bias[l], moe_w_gate, moe_w_up, moe_w_down, l,
                             sh_w_gate[l], sh_w_up[l], sh_w_down[l], ln2_g[l], ln2_b[l])
    return h32.reshape(batch, seq, d)
```

```python
import functools
import math

import jax
import jax.numpy as jnp
import numpy as np
from jax import lax
from jax.experimental import pallas as pl
from jax.experimental.pallas import tpu as pltpu

F32 = jnp.float32
BF16 = jnp.bfloat16

D_MODEL = 2048
DEPTH = 2
CHUNK = 64

HG_HEADS = 8
HG_D = 128
HG_W = HG_HEADS * HG_D
RW_HEADS = 16
RW_N = 64
RW_W = RW_HEADS * RW_N
RW_DECAY_RANK = 64
RW_A_RANK = 64
RW_GATE_RANK = 160
RW_GN_EPS = 64e-5
RW_LOW_PAD = 512

SSD_INNER = 2 * D_MODEL
SSD_P = 64
SSD_HEADS = SSD_INNER // SSD_P
SSD_GROUPS = 8
SSD_HPG = SSD_HEADS // SSD_GROUPS
SSD_N = 128
SSD_CONV = 4
SSD_BC = SSD_GROUPS * SSD_N
SSD_CONV_DIM = SSD_INNER + 2 * SSD_BC
SSD_GW = SSD_HPG * SSD_P

N_EXPERTS = 64
TOP_K = 8
N_GROUPS = 8
TOPK_GROUPS = 4
EXPERT_DIM = 512
ROUTED_SCALE = 2.5

NORM_EPS = 1e-5
DN_ALPHA = (2 * DEPTH) ** 0.25

VMEM_LIMIT = 56 * 1024 * 1024


def _cparams(*sem):
    return pltpu.CompilerParams(dimension_semantics=sem, vmem_limit_bytes=VMEM_LIMIT)


def _const_spec(shape):
    nd = len(shape)
    return pl.BlockSpec(shape, lambda *_: (0,) * nd, pipeline_mode=pl.Buffered(1))


def _split3(x):
    hi = x.astype(BF16)
    r1 = x - hi.astype(F32)
    mid = r1.astype(BF16)
    lo = (r1 - mid.astype(F32)).astype(BF16)
    return hi, mid, lo


def _dot(a, b):
    return jnp.dot(a, b, preferred_element_type=F32)


def _dot_exact_lhs(m_bf16, x):
    hi, mid, lo = _split3(x)
    return _dot(m_bf16, hi) + _dot(m_bf16, mid) + _dot(m_bf16, lo)


def _dot_exact_rhs(x, m_bf16):
    hi, mid, lo = _split3(x)
    return _dot(hi, m_bf16) + _dot(mid, m_bf16) + _dot(lo, m_bf16)


def _sigmoid(x):
    return 1.0 / (1.0 + jnp.exp(-x))


def _silu(x):
    return x * _sigmoid(x)


def _softplus(x):
    return jnp.maximum(x, 0.0) + jnp.log(1.0 + jnp.exp(-jnp.abs(x)))


def _mm_kernel(x_ref, w_ref, o_ref):
    o_ref[...] = _dot(x_ref[...], w_ref[...]).astype(o_ref.dtype)


def matmul(x, w, out_dtype, tm=1024, tn=512):
    m, k = x.shape
    n = w.shape[1]
    tm = min(tm, m)
    tn = min(tn, n)
    assert m % tm == 0 and n % tn == 0
    return pl.pallas_call(
        _mm_kernel,
        grid=(m // tm, n // tn),
        in_specs=[pl.BlockSpec((tm, k), lambda i, j: (i, 0)),
                  pl.BlockSpec((k, tn), lambda i, j: (0, j))],
        out_specs=pl.BlockSpec((tm, tn), lambda i, j: (i, j)),
        out_shape=jax.ShapeDtypeStruct((m, n), out_dtype),
        compiler_params=_cparams("parallel", "arbitrary"),
        name="matmul",
    )(x, w)


def _layer_norm_rows(y, g, b):
    mu = jnp.mean(y, axis=-1, keepdims=True)
    yc = y - mu
    var = jnp.mean(yc * yc, axis=-1, keepdims=True)
    return yc * lax.rsqrt(var + NORM_EPS) * g + b


def _bf16_bits(x):
    u = lax.bitcast_convert_type(x, jnp.uint32)
    u = u + jnp.uint32(0x7FFF) + ((u >> 16) & jnp.uint32(1))
    return u & jnp.uint32(0xFFFF0000)


def _pack_halves(lo, hi):
    return (_bf16_bits(lo) >> 16) | _bf16_bits(hi)


def _unpack_halves(u):
    lo = lax.bitcast_convert_type(u << 16, F32)
    hi = lax.bitcast_convert_type(u & jnp.uint32(0xFFFF0000), F32)
    return lo, hi


HALF = D_MODEL // 2
ROW_SUB = 8
ROW_LANES = HALF // ROW_SUB


def _store_row_tiles(ref, packed):
    n = packed.shape[0]
    for c in range(ROW_SUB):
        ref[pl.ds(c, n, stride=ROW_SUB), :] = packed[:, c * ROW_LANES:(c + 1) * ROW_LANES]


def _load_row_tiles(ref, n):
    return jnp.concatenate([ref[pl.ds(c, n, stride=ROW_SUB), :] for c in range(ROW_SUB)], axis=1)


def _ln_outputs(y, g_ref, b_ref, o32_ref, o16_ref, opk_ref):
    o = _layer_norm_rows(y, g_ref[...], b_ref[...])
    o32_ref[...] = o
    o16_ref[...] = o.astype(BF16)
    _store_row_tiles(opk_ref, _pack_halves(o[:, :HALF], o[:, HALF:]))


def _mm_ln_kernel(x_ref, w_ref, h_ref, g_ref, b_ref, o32_ref, o16_ref, opk_ref):
    y = _dot(x_ref[...], w_ref[...]) + DN_ALPHA * h_ref[...]
    _ln_outputs(y, g_ref, b_ref, o32_ref, o16_ref, opk_ref)


def _mm2_ln_kernel(xa_ref, xb_ref, gate_ref, w_ref, h_ref, g_ref, b_ref, o32_ref, o16_ref, opk_ref):
    ka = xa_ref.shape[1]
    xb = (xb_ref[...] * gate_ref[...]).astype(BF16)
    y = _dot(xa_ref[...], w_ref[:ka, :]) + _dot(xb, w_ref[ka:, :]) + DN_ALPHA * h_ref[...]
    _ln_outputs(y, g_ref, b_ref, o32_ref, o16_ref, opk_ref)


def matmul_residual_ln(xs, w, h, g, b, tm=256):
    m = xs[0].shape[0]
    k, d = w.shape
    tm = min(tm, m)
    assert m % tm == 0
    row = lambda a: pl.BlockSpec((tm, a.shape[1]), lambda i: (i, 0))
    return pl.pallas_call(
        _mm_ln_kernel if len(xs) == 1 else _mm2_ln_kernel,
        grid=(m // tm,),
        in_specs=[row(a) for a in xs] + [_const_spec((k, d)), row(h), _const_spec((1, d)), _const_spec((1, d))],
        out_specs=[pl.BlockSpec((tm, d), lambda i: (i, 0)),
                   pl.BlockSpec((tm, d), lambda i: (i, 0)),
                   pl.BlockSpec((tm * ROW_SUB, ROW_LANES), lambda i: (i, 0))],
        out_shape=[jax.ShapeDtypeStruct((m, d), F32), jax.ShapeDtypeStruct((m, d), BF16),
                   jax.ShapeDtypeStruct((m * ROW_SUB, ROW_LANES), jnp.uint32)],
        compiler_params=_cparams("parallel"),
        name="matmul_residual_ln",
    )(*xs, w, h, g.reshape(1, d), b.reshape(1, d))


HG_LEVELS = (32, 16, 8, 4, 2, 1)


def _hgrn2_constants():
    t = np.arange(CHUNK)[:, None]
    u = np.arange(CHUNK)[None, :]
    blocks = [(u <= t)]
    for b in HG_LEVELS:
        start = (t // b) * b
        end = start + b - 1
        blocks.append((u >= start) & (u <= t))
        blocks.append((u > t) & (u <= end))
    blocks.append(u > t)
    cm = np.concatenate(blocks, axis=0).astype(np.float32)
    s = u
    masks = [((t // (2 * b)) == (s // (2 * b))) & ((t // b) % 2 == 1) & ((s // b) % 2 == 0) for b in HG_LEVELS]
    return cm, np.stack(masks).astype(np.float32)


HG_HPS = 8


def _hgrn2_kernel(q_ref, f_ref, i_ref, g_ref, lb_ref, gnw_ref, cm_ref, mask_ref, o_ref, st_ref, *, n_chunks):
    @pl.when(pl.program_id(2) == 0)
    def _():
        st_ref[...] = jnp.zeros_like(st_ref)

    lb = lb_ref[...]
    cm = cm_ref[...]
    nt = (((1,), (1,)), ((), ()))
    tail = CHUNK + 2 * CHUNK * len(HG_LEVELS)

    def chunk(c, carry):
        rows = pl.ds(pl.multiple_of(c * CHUNK, CHUNK), CHUNK)
        fp_all = f_ref[rows, :]
        lf_all = jnp.log(lb + (1.0 - lb) * _sigmoid(fp_all))
        k_all = (1.0 - lb) * _sigmoid(-fp_all)
        seg_all = _dot_exact_lhs(cm, lf_all)
        for hh in range(HG_HPS):
            cols = slice(hh * HG_D, (hh + 1) * HG_D)
            q = q_ref[rows, cols]
            v = i_ref[rows, cols]
            g = g_ref[rows, cols]
            k = k_all[:, cols]
            seg = seg_all[:, cols]
            bc = seg[0:CHUNK]
            st = st_ref[hh]
            o = lax.dot_general((q * jnp.exp(bc)).astype(BF16), st.astype(BF16), nt, preferred_element_type=F32)
            att = jnp.zeros((CHUNK, CHUNK), F32)
            for li in range(len(HG_LEVELS)):
                base = CHUNK + 2 * CHUNK * li
                qs = q * jnp.exp(seg[base:base + CHUNK])
                ks = k * jnp.exp(seg[base + CHUNK:base + 2 * CHUNK])
                a_l = lax.dot_general(qs.astype(BF16), ks.astype(BF16), nt, preferred_element_type=F32)
                att = att + a_l * mask_ref[li]
            o = o + _dot(att.astype(BF16), v.astype(BF16)) + jnp.sum(q * k, axis=-1, keepdims=True) * v
            kd = k * jnp.exp(seg[tail:tail + CHUNK])
            st_ref[hh] = st * jnp.exp(bc[CHUNK - 1:CHUNK]) + _dot(v.T.astype(BF16), kd.astype(BF16))
            ms = jnp.mean(o * o, axis=-1, keepdims=True)
            o_ref[rows, cols] = (o * lax.rsqrt(ms + NORM_EPS) * gnw_ref[:, cols] * _silu(g)).astype(o_ref.dtype)
        return carry

    lax.fori_loop(0, n_chunks, chunk, 0)


def hgrn2(p_hg, lb, gn_w, batch, seq, tb=256):
    tb = min(tb, seq)
    nt = seq // tb
    hg = HG_HEADS // HG_HPS
    w = HG_HPS * HG_D
    cm, masks = _hgrn2_constants()
    spec = lambda off: pl.BlockSpec((tb, w), lambda b, h, t: (b * nt + t, off * hg + h))
    return pl.pallas_call(
        functools.partial(_hgrn2_kernel, n_chunks=tb // CHUNK),
        grid=(batch, hg, nt),
        in_specs=[spec(0), spec(1), spec(2), spec(3),
                  pl.BlockSpec((1, w), lambda b, h, t: (0, h)),
                  pl.BlockSpec((1, w), lambda b, h, t: (0, h)),
                  _const_spec(cm.shape), _const_spec(masks.shape)],
        out_specs=pl.BlockSpec((tb, w), lambda b, h, t: (b * nt + t, h)),
        out_shape=jax.ShapeDtypeStruct((batch * seq, HG_W), BF16),
        scratch_shapes=[pltpu.VMEM((HG_HPS, HG_D, HG_D), F32)],
        compiler_params=_cparams("parallel", "parallel", "arbitrary"),
        name="hgrn2",
    )(p_hg, p_hg, p_hg, p_hg, lb.reshape(1, HG_W), gn_w.reshape(1, HG_W),
      jnp.asarray(cm, BF16), jnp.asarray(masks))


def _rwkv_prep_kernel(pm_ref, pmprev_ref, plo_ref, ploprev_ref, mixm_ref, mixl_ref, w0_ref, w2_ref, a0_ref,
                      a2_ref, g2_ref, r_o, k_o, v_o, w_o, a_o, g_o, *, tb, seq):
    first = (pl.program_id(0) * tb) % seq == 0
    keep = jnp.where(first, 0.0, 1.0)

    def shift_mix(cur, prev_blk, mix):
        row = lax.broadcasted_iota(jnp.int32, cur.shape, 0)
        shifted = jnp.where(row == 0, prev_blk[7:8, :] * keep, pltpu.roll(cur, 1, 0))
        return cur + (shifted - cur) * mix

    for idx, out in enumerate((r_o, k_o, v_o)):
        cols = slice(idx * RW_W, (idx + 1) * RW_W)
        out[...] = shift_mix(pm_ref[:, cols], pmprev_ref[:, cols], mixm_ref[:, cols])
    low = shift_mix(plo_ref[...], ploprev_ref[...], mixl_ref[...])
    wl = jnp.tanh(low[:, 0:128]).astype(BF16)
    al = low[:, 128:256].astype(BF16)
    gl = _sigmoid(low[:, 256:512]).astype(BF16)
    w_log = -_softplus(-(w0_ref[...] + _dot(wl, w2_ref[...]))) - 0.5
    w_o[...] = jnp.exp(-jnp.exp(w_log))
    a_o[...] = _sigmoid(a0_ref[...] + _dot(al, a2_ref[...]))
    g_o[...] = _dot(gl, g2_ref[...])


def rwkv_prep(p_main, p_low, mix_main, mix_low, w0, w2p, a0, a2p, g2p, seq, tb=256):
    m = p_main.shape[0]
    tb = min(tb, seq)
    prev = lambda i: (jnp.maximum(i * (tb // 8) - 1, 0), 0)
    out = jax.ShapeDtypeStruct((m, RW_W), F32)
    ospec = pl.BlockSpec((tb, RW_W), lambda i: (i, 0))
    return pl.pallas_call(
        functools.partial(_rwkv_prep_kernel, tb=tb, seq=seq),
        grid=(m // tb,),
        in_specs=[pl.BlockSpec((tb, 3 * RW_W), lambda i: (i, 0)),
                  pl.BlockSpec((8, 3 * RW_W), prev),
                  pl.BlockSpec((tb, RW_LOW_PAD), lambda i: (i, 0)),
                  pl.BlockSpec((8, RW_LOW_PAD), prev),
                  _const_spec((1, 3 * RW_W)), _const_spec((1, RW_LOW_PAD)),
                  _const_spec((1, RW_W)), _const_spec((128, RW_W)),
                  _const_spec((1, RW_W)), _const_spec((128, RW_W)),
                  _const_spec((256, RW_W))],
        out_specs=[ospec] * 6,
        out_shape=[out] * 6,
        compiler_params=_cparams("parallel"),
        name="rwkv_prep",
    )(p_main, p_main, p_low, p_low, mix_main, mix_low, w0, w2p, a0, a2p, g2p)


def _rwkv_scan_kernel(r_ref, k_ref, v_ref, w_ref, a_ref, kkp_ref, kap_ref, rkp_ref, lnw_ref, lnb_ref,
                      o_ref, s_ref, vec_ref, *, tt):
    @pl.when(pl.program_id(0) == 0)
    def _():
        s_ref[...] = jnp.zeros_like(s_ref)

    kkp = kkp_ref[...]
    kap = kap_ref[...]
    rkp = rkp_ref[...]
    lnw = lnw_ref[...]
    lnb = lnb_ref[...]

    def step(t, carry):
        kt = k_ref[t]
        at = a_ref[t]
        rt = r_ref[t]
        vt = v_ref[t]
        kk = kt * kkp
        nrm = jnp.sqrt(jnp.sum(kk * kk, axis=0, keepdims=True))
        kk = kk / jnp.maximum(nrm, 1e-12)
        kh = kt * (1.0 + (at - 1.0) * kap)
        vec_ref[0] = -kk
        vec_ref[1] = kk * at
        vec_ref[2] = kh
        sa = jnp.zeros((RW_N, kt.shape[1]), F32)
        for j in range(RW_N):
            sa = sa + s_ref[j] * vec_ref[0, j:j + 1, :]
        y = jnp.zeros_like(sa)
        for j in range(RW_N):
            sj = (s_ref[j] * w_ref[t, j:j + 1, :] + sa * vec_ref[1, j:j + 1, :]
                  + vt * vec_ref[2, j:j + 1, :])
            s_ref[j] = sj
            y = y + sj * r_ref[t, j:j + 1, :]
        mu = jnp.mean(y, axis=0, keepdims=True)
        yc = y - mu
        var = jnp.mean(yc * yc, axis=0, keepdims=True)
        yn = yc * lax.rsqrt(var + RW_GN_EPS) * lnw + lnb
        bonus = jnp.sum(rt * kh * rkp, axis=0, keepdims=True) * vt
        o_ref[t] = yn + bonus
        return carry

    lax.fori_loop(0, tt, step, 0, unroll=2)


def rwkv_scan(r, k, v, w, a, kkp, kap, rkp, lnw, lnb, tt=16):
    t, n, lanes = r.shape
    tt = min(tt, t)
    blk = pl.BlockSpec((tt, n, lanes), lambda i: (i, 0, 0))
    par = _const_spec((n, lanes))
    return pl.pallas_call(
        functools.partial(_rwkv_scan_kernel, tt=tt),
        grid=(t // tt,),
        in_specs=[blk] * 5 + [par] * 5,
        out_specs=blk,
        out_shape=jax.ShapeDtypeStruct((t, n, lanes), F32),
        scratch_shapes=[pltpu.VMEM((n, n, lanes), F32), pltpu.VMEM((3, n, lanes), F32)],
        compiler_params=_cparams("arbitrary"),
        name="rwkv_scan",
    )(r, k, v, w, a, kkp, kap, rkp, lnw, lnb)


def _time_major(x, batch, seq):
    return x.reshape(batch, seq, RW_HEADS, RW_N).transpose(1, 3, 0, 2).reshape(seq, RW_N, batch * RW_HEADS)


def _token_major(x, batch, seq):
    return x.reshape(seq, RW_N, batch, RW_HEADS).transpose(2, 0, 3, 1).reshape(batch * seq, RW_W)


def _head_param(p, batch):
    return jnp.tile(p.reshape(RW_HEADS, RW_N).T, (1, batch))


def _pad_rows(w, rows):
    return jnp.pad(w, ((0, rows - w.shape[0]), (0, 0)))


def rwkv7(p_main, p_low, mix, w0, w2, a0, a2, g2, k_k, k_a, r_k, ln_w, ln_b, batch, seq):
    mix_main = mix[:3 * RW_W].reshape(1, -1)
    lo = mix[3 * RW_W:]
    c1 = RW_DECAY_RANK
    c2 = c1 + RW_A_RANK
    mix_low = jnp.concatenate([jnp.pad(lo[:c1], (0, 128 - RW_DECAY_RANK)),
                               jnp.pad(lo[c1:c2], (0, 128 - RW_A_RANK)),
                               jnp.pad(lo[c2:], (0, 256 - RW_GATE_RANK))]).reshape(1, RW_LOW_PAD)
    r, k, v, w, a, g = rwkv_prep(
        p_main, p_low, mix_main, mix_low, w0.reshape(1, -1), _pad_rows(w2, 128).astype(BF16),
        a0.reshape(1, -1), _pad_rows(a2, 128).astype(BF16), _pad_rows(g2, 256).astype(BF16), seq)
    tm = lambda x: _time_major(x, batch, seq)
    hp = lambda p: _head_param(p.reshape(-1), batch)
    o = rwkv_scan(tm(r), tm(k), tm(v), tm(w), tm(a), hp(k_k), hp(k_a), hp(r_k), hp(ln_w), hp(ln_b))
    return _token_major(o, batch, seq), g


def _ssd_kernel(z_ref, xbc_ref, dt_ref, cw_ref, cb_ref, dtb_ref, alog_ref, dexp_ref, nw_ref, ex_ref, tri_ref,
                o_ref, xpad_ref, xact_ref, st_ref, y_ref):
    c = pl.program_id(1)

    @pl.when(c == 0)
    def _():
        st_ref[...] = jnp.zeros_like(st_ref)
        xpad_ref[0:8, :] = jnp.zeros((8, SSD_CONV_DIM), F32)

    xpad_ref[8:8 + CHUNK, :] = xbc_ref[...]
    ct = 512
    for j in range(SSD_CONV_DIM // ct):
        cols = slice(j * ct, (j + 1) * ct)
        win = xpad_ref[:, cols]
        acc = cw_ref[0:1, cols] * win
        for tap in range(1, SSD_CONV):
            acc = pltpu.roll(acc, 1, 0) + cw_ref[tap:tap + 1, cols] * win
        xact_ref[:, cols] = _silu(acc[8:] + cb_ref[:, cols])
    xpad_ref[0:8, :] = xpad_ref[CHUNK:CHUNK + 8, :]

    dtv = _softplus(dt_ref[...] + dtb_ref[...])
    da = dtv * (-jnp.exp(alog_ref[...]))
    acum = _dot_exact_lhs(tri_ref[...], da)
    acum_t = acum.T
    ex = ex_ref[...]
    dtx = _dot_exact_rhs(dtv, ex)
    e_in = _dot_exact_rhs(jnp.exp(acum), ex)
    e_out = _dot_exact_rhs(jnp.exp(acum[CHUNK - 1:CHUNK, :] - acum), ex)
    e_last = e_in[CHUNK - 1:CHUNK, :]
    xs = xact_ref[:, 0:SSD_INNER]
    xdt = xs * dtx
    ti = lax.broadcasted_iota(jnp.int32, (CHUNK, CHUNK), 0)
    si = lax.broadcasted_iota(jnp.int32, (CHUNK, CHUNK), 1)
    causal = ti >= si
    lane = lax.broadcasted_iota(jnp.int32, (CHUNK, 128), 1)
    nt = (((1,), (1,)), ((), ()))

    for g in range(SSD_GROUPS):
        bg = xact_ref[:, SSD_INNER + g * SSD_N:SSD_INNER + (g + 1) * SSD_N]
        cg = xact_ref[:, SSD_INNER + SSD_BC + g * SSD_N:SSD_INNER + SSD_BC + (g + 1) * SSD_N]
        cgb = cg.astype(BF16)
        cb = lax.dot_general(cgb, bg.astype(BF16), nt, preferred_element_type=F32)
        gc = slice(g * SSD_GW, (g + 1) * SSD_GW)
        sgt = st_ref[g]
        y_g = _dot(cgb, sgt.astype(BF16)) * e_in[:, gc]
        xdt_g = xdt[:, gc]
        for m in range(SSD_HPG // 2):
            xp = xdt_g[:, m * 128:(m + 1) * 128]
            yp = y_g[:, m * 128:(m + 1) * 128]
            for half in range(2):
                h = g * SSD_HPG + 2 * m + half
                seg = acum[:, h:h + 1] - acum_t[h:h + 1, :]
                lmat = jnp.where(causal, jnp.exp(jnp.minimum(seg, 0.0)), 0.0)
                rhs = jnp.where((lane >= 64) == (half == 1), xp, 0.0)
                yp = yp + _dot((cb * lmat).astype(BF16), rhs.astype(BF16))
            y_ref[:, g * SSD_GW + m * 128:g * SSD_GW + (m + 1) * 128] = yp
        st_ref[g] = sgt * e_last[:, gc] + _dot(bg.T.astype(BF16), (xdt_g * e_out[:, gc]).astype(BF16))

    for g in range(SSD_GROUPS):
        gc = slice(g * SSD_GW, (g + 1) * SSD_GW)
        y = y_ref[:, gc] + dexp_ref[:, gc] * xact_ref[:, gc]
        y = y * _silu(z_ref[:, gc].astype(F32))
        ms = jnp.mean(y * y, axis=-1, keepdims=True)
        o_ref[:, gc] = (y * lax.rsqrt(ms + NORM_EPS) * nw_ref[:, gc]).astype(o_ref.dtype)


def ssd(z, xbc, dt, conv_w, conv_b, dt_bias, a_log, d_skip, norm_w, batch, seq):
    nc = seq // CHUNK
    pad = lambda v: jnp.pad(v.reshape(1, -1), ((0, 0), (0, 128 - SSD_HEADS)))
    ex = np.zeros((128, SSD_INNER), np.float32)
    for h in range(SSD_HEADS):
        ex[h, h * SSD_P:(h + 1) * SSD_P] = 1.0
    tri = np.tril(np.ones((CHUNK, CHUNK), np.float32))
    row = lambda w: pl.BlockSpec((CHUNK, w), lambda b, c: (b * nc + c, 0))
    return pl.pallas_call(
        _ssd_kernel,
        grid=(batch, nc),
        in_specs=[row(SSD_INNER), row(SSD_CONV_DIM), row(128),
                  _const_spec((SSD_CONV, SSD_CONV_DIM)), _const_spec((1, SSD_CONV_DIM)),
                  _const_spec((1, 128)), _const_spec((1, 128)),
                  _const_spec((1, SSD_INNER)), _const_spec((1, SSD_INNER)),
                  _const_spec((128, SSD_INNER)), _const_spec((CHUNK, CHUNK))],
        out_specs=row(SSD_INNER),
        out_shape=jax.ShapeDtypeStruct((batch * seq, SSD_INNER), BF16),
        scratch_shapes=[pltpu.VMEM((CHUNK + 8, SSD_CONV_DIM), F32),
                        pltpu.VMEM((CHUNK, SSD_CONV_DIM), F32),
                        pltpu.VMEM((SSD_GROUPS, SSD_N, SSD_GW), F32),
                        pltpu.VMEM((CHUNK, SSD_INNER), F32)],
        compiler_params=_cparams("parallel", "arbitrary"),
        name="ssd",
    )(z, xbc, dt, conv_w, conv_b.reshape(1, -1), pad(dt_bias), pad(a_log),
      jnp.repeat(d_skip, SSD_P).reshape(1, -1), norm_w.reshape(1, -1), jnp.asarray(ex, BF16),
      jnp.asarray(tri, BF16))


FFN_TM = 256


def _router_kernel(x_ref, rt_ref, bias_ref, up_ref, ls_ref, eid_o, rank_o, gate_o, cnt_o, carry_ref, *, tb):
    @pl.when(pl.program_id(0) == 0)
    def _():
        carry_ref[...] = jnp.zeros_like(carry_ref)

    nt = (((1,), (1,)), ((), ()))
    x = x_ref[...]
    rt = rt_ref[...]
    xh = x.astype(BF16)
    xl = (x - xh.astype(F32)).astype(BF16)
    rh = rt.astype(BF16)
    rl = (rt - rh.astype(F32)).astype(BF16)
    dg = lambda a, b: lax.dot_general(a, b, nt, preferred_element_type=F32)
    logits = dg(rh, xh) + dg(rh, xl) + dg(rl, xh)
    scores = _sigmoid(logits)
    reps = tb // 128
    wide = lambda a: jnp.concatenate([a] * reps, axis=1) if reps > 1 else a
    biased = scores + wide(bias_ref[...])
    neg = -jnp.inf

    io8 = lax.broadcasted_iota(jnp.int32, (8, tb), 0)
    blocks, gs = [], []
    for g in range(N_GROUPS):
        blk = biased[8 * g:8 * g + 8, :]
        m1 = jnp.max(blk, axis=0, keepdims=True)
        first = jnp.min(jnp.where(blk == m1, io8, 8), axis=0, keepdims=True)
        m2 = jnp.max(jnp.where(io8 == first, neg, blk), axis=0, keepdims=True)
        blocks.append(blk)
        gs.append(m1 + m2)
    masked = []
    for g in range(N_GROUPS):
        ahead = jnp.zeros((1, tb), jnp.int32)
        for o in range(N_GROUPS):
            if o == g:
                continue
            beats = (gs[o] > gs[g]) | ((gs[o] == gs[g]) & (o < g))
            ahead = ahead + jnp.where(beats, 1, 0)
        masked.append(jnp.where(ahead < TOPK_GROUPS, blocks[g], neg))
    masked = jnp.concatenate(masked, axis=0)
    eidx = lax.broadcasted_iota(jnp.int32, (N_EXPERTS, tb), 0)
    ahead = jnp.zeros((N_EXPERTS, tb), jnp.int32)
    for o in range(N_EXPERTS):
        row = masked[o:o + 1, :]
        beats = (row > masked) | ((row == masked) & (eidx > o))
        ahead = ahead + jnp.where(beats, 1, 0)
    sel = ahead < TOP_K
    self_ = jnp.where(sel, 1.0, 0.0)
    wts = jnp.where(sel, scores, 0.0)
    gate = wts / jnp.sum(wts, axis=0, keepdims=True) * ROUTED_SCALE

    selb = self_.astype(BF16)
    carry = carry_ref[...]
    rank = _dot(selb, up_ref[...]) + wide(carry)
    new_carry = carry + _dot(selb, jnp.ones((tb, 128), BF16))
    carry_ref[...] = new_carry
    cnt_o[...] = new_carry
    before = _dot(ls_ref[...], selb)
    eidf = eidx.astype(F32)
    for j in range(TOP_K):
        hit = sel & (before == float(j))
        pick = lambda a: jnp.sum(jnp.where(hit, a, 0.0), axis=0, keepdims=True)
        eid_o[j:j + 1, :] = pick(eidf).astype(jnp.int32)
        rank_o[j:j + 1, :] = pick(rank).astype(jnp.int32)
        gate_o[j:j + 1, :] = pick(gate)


def moe_route(h32, router, bias, tb=256):
    m = h32.shape[0]
    tb = min(tb, m)
    up = np.triu(np.ones((tb, tb), np.float32), k=1)
    ls = np.tril(np.ones((N_EXPERTS, N_EXPERTS), np.float32), k=-1)
    slot = lambda dt: jax.ShapeDtypeStruct((TOP_K, m), dt)
    return pl.pallas_call(
        functools.partial(_router_kernel, tb=tb),
        grid=(m // tb,),
        in_specs=[pl.BlockSpec((tb, D_MODEL), lambda i: (i, 0)),
                  _const_spec((N_EXPERTS, D_MODEL)), _const_spec((N_EXPERTS, 128)),
                  _const_spec((tb, tb)), _const_spec((N_EXPERTS, N_EXPERTS))],
        out_specs=[pl.BlockSpec((TOP_K, tb), lambda i: (0, i))] * 3
                  + [pl.BlockSpec((N_EXPERTS, 128), lambda i: (0, 0))],
        out_shape=[slot(jnp.int32), slot(jnp.int32), slot(F32),
                   jax.ShapeDtypeStruct((N_EXPERTS, 128), F32)],
        scratch_shapes=[pltpu.VMEM((N_EXPERTS, 128), F32)],
        compiler_params=_cparams("arbitrary"),
        name="moe_route",
    )(h32, router.T, jnp.broadcast_to(bias.reshape(N_EXPERTS, 1), (N_EXPERTS, 128)),
      jnp.asarray(up, BF16), jnp.asarray(ls, BF16))


def _frame_rows(p):
    return pl.ds(pl.multiple_of(p * ROW_SUB, ROW_SUB), ROW_SUB)


def _dispatch_kernel(seg_ref, ends_ref, pos_ref, x_ref, xs_out, zero_ref, sem, zsem, *, tb):
    tile_rows = FFN_TM * ROW_SUB

    @pl.when(pl.program_id(0) == 0)
    def _():
        zero_ref[...] = jnp.zeros_like(zero_ref)

        def fill(e):
            start = pl.multiple_of((ends_ref[e] - FFN_TM) * ROW_SUB, tile_rows)
            return pltpu.make_async_copy(zero_ref, xs_out.at[pl.ds(start, tile_rows), :], zsem)

        def start_fill(e, c):
            @pl.when(seg_ref[e] > 0)
            def _():
                fill(e).start()
            return c

        def wait_fill(e, c):
            @pl.when(seg_ref[e] > 0)
            def _():
                fill(e).wait()
            return c

        lax.fori_loop(0, N_EXPERTS, start_fill, 0)
        lax.fori_loop(0, N_EXPERTS, wait_fill, 0)

    def copy(t, p):
        return pltpu.make_async_copy(x_ref.at[_frame_rows(t), :], xs_out.at[_frame_rows(p), :], sem)

    def issue(t, c):
        for j in range(TOP_K):
            copy(t, pos_ref[j, t]).start(priority=j % 2)
        return c

    lax.fori_loop(0, tb, issue, 0)

    def drain(t, c):
        for j in range(TOP_K):
            copy(0, 0).wait()
        return c

    lax.fori_loop(0, tb, drain, 0)


def moe_dispatch(xpk, pos8, seg, ends, n_rows, tb=256):
    m = xpk.shape[0] // ROW_SUB
    tb = min(tb, m)
    grid_spec = pltpu.PrefetchScalarGridSpec(
        num_scalar_prefetch=2,
        grid=(m // tb,),
        in_specs=[pl.BlockSpec((TOP_K, tb), lambda i, sg, en: (0, i), memory_space=pltpu.SMEM),
                  pl.BlockSpec((tb * ROW_SUB, ROW_LANES), lambda i, sg, en: (i, 0))],
        out_specs=pl.BlockSpec(memory_space=pl.ANY),
        scratch_shapes=[pltpu.VMEM((FFN_TM * ROW_SUB, ROW_LANES), xpk.dtype),
                        pltpu.SemaphoreType.DMA(()), pltpu.SemaphoreType.DMA(())],
    )
    return pl.pallas_call(
        functools.partial(_dispatch_kernel, tb=tb),
        grid_spec=grid_spec,
        out_shape=jax.ShapeDtypeStruct((n_rows * ROW_SUB, ROW_LANES), xpk.dtype),
        compiler_params=_cparams("arbitrary"),
        name="moe_dispatch",
    )(seg, ends, pos8, xpk)


def _ffn_kernel(te_ref, nu_ref, xs_ref, wg_ref, wu_ref, wd_ref, ys_ref, wg16_ref, wu16_ref, wd16_ref):
    i = pl.program_id(0)

    @pl.when((i == 0) | (te_ref[i] != te_ref[jnp.maximum(i - 1, 0)]))
    def _():
        wg16_ref[...] = wg_ref[0].astype(BF16)
        wu16_ref[...] = wu_ref[0].astype(BF16)
        wd16_ref[...] = wd_ref[0].astype(BF16)

    @pl.when(i < nu_ref[0])
    def _():
        lo, hi = _unpack_halves(_load_row_tiles(xs_ref, FFN_TM))
        lo = lo.astype(BF16)
        hi = hi.astype(BF16)
        hg = _dot(lo, wg16_ref[:HALF, :]) + _dot(hi, wg16_ref[HALF:, :])
        hu = _dot(lo, wu16_ref[:HALF, :]) + _dot(hi, wu16_ref[HALF:, :])
        y = _dot((_silu(hg) * hu).astype(BF16), wd16_ref[...])
        _store_row_tiles(ys_ref, _pack_halves(y[:, :HALF], y[:, HALF:]))


def moe_experts(xs, tile_expert, n_used, w_gate, w_up, w_down, layer):
    n_tiles = xs.shape[0] // (FFN_TM * ROW_SUB)
    wspec = lambda shp: pl.BlockSpec((None, 1) + shp, lambda i, te, nu: (layer, te[i], 0, 0))
    rows = pl.BlockSpec((FFN_TM * ROW_SUB, ROW_LANES), lambda i, te, nu: (jnp.minimum(i, nu[0] - 1), 0))
    grid_spec = pltpu.PrefetchScalarGridSpec(
        num_scalar_prefetch=2,
        grid=(n_tiles,),
        in_specs=[rows, wspec((D_MODEL, EXPERT_DIM)), wspec((D_MODEL, EXPERT_DIM)),
                  wspec((EXPERT_DIM, D_MODEL))],
        out_specs=rows,
        scratch_shapes=[pltpu.VMEM((D_MODEL, EXPERT_DIM), BF16), pltpu.VMEM((D_MODEL, EXPERT_DIM), BF16),
                        pltpu.VMEM((EXPERT_DIM, D_MODEL), BF16)],
    )
    return pl.pallas_call(
        _ffn_kernel,
        grid_spec=grid_spec,
        out_shape=jax.ShapeDtypeStruct(xs.shape, xs.dtype),
        compiler_params=_cparams("arbitrary"),
        name="moe_experts",
    )(tile_expert, n_used, xs, w_gate, w_up, w_down)


def _combine_kernel(pos_ref, gate_ref, h32_ref, h16_ref, sg_ref, su_ref, sd_ref, lng_ref, lnb_ref, ys_hbm,
                    o32_ref, o16_ref, buf_ref, sem, *, tb):
    def copy(p, j, t):
        return pltpu.make_async_copy(ys_hbm.at[_frame_rows(p), :], buf_ref.at[j, _frame_rows(t), :], sem)

    def issue(t, c):
        for j in range(TOP_K):
            copy(pos_ref[j, t], j, t).start(priority=j % 2)
        return c

    lax.fori_loop(0, tb, issue, 0)

    x = h16_ref[...]
    hdn = (_silu(_dot(x, sg_ref[...])) * _dot(x, su_ref[...])).astype(BF16)
    shared = _dot(hdn, sd_ref[...])
    acc_lo = shared[:, :HALF] + DN_ALPHA * h32_ref[:, :HALF]
    acc_hi = shared[:, HALF:] + DN_ALPHA * h32_ref[:, HALF:]

    def drain(t, c):
        for j in range(TOP_K):
            copy(0, 0, 0).wait()
        return c

    lax.fori_loop(0, tb, drain, 0)

    for j in range(TOP_K):
        lo, hi = _unpack_halves(_load_row_tiles(buf_ref.at[j], tb))
        gj = gate_ref[:, j:j + 1]
        acc_lo = acc_lo + gj * lo
        acc_hi = acc_hi + gj * hi
    mu = (jnp.sum(acc_lo, axis=-1, keepdims=True) + jnp.sum(acc_hi, axis=-1, keepdims=True)) / D_MODEL
    c_lo = acc_lo - mu
    c_hi = acc_hi - mu
    var = (jnp.sum(c_lo * c_lo, axis=-1, keepdims=True) + jnp.sum(c_hi * c_hi, axis=-1, keepdims=True)) / D_MODEL
    inv = lax.rsqrt(var + NORM_EPS)
    o_lo = c_lo * inv * lng_ref[:, :HALF] + lnb_ref[:, :HALF]
    o_hi = c_hi * inv * lng_ref[:, HALF:] + lnb_ref[:, HALF:]
    o32_ref[:, :HALF] = o_lo
    o32_ref[:, HALF:] = o_hi
    o16_ref[:, :HALF] = o_lo.astype(BF16)
    o16_ref[:, HALF:] = o_hi.astype(BF16)


def moe_combine(ys, pos8, gate_t, h32, h16, s_gate, s_up, s_down, ln_g, ln_b, tb=128):
    m, d = h32.shape
    tb = min(tb, m)
    row = lambda w: pl.BlockSpec((tb, w), lambda i: (i, 0))
    return pl.pallas_call(
        functools.partial(_combine_kernel, tb=tb),
        grid=(m // tb,),
        in_specs=[pl.BlockSpec((TOP_K, tb), lambda i: (0, i), memory_space=pltpu.SMEM),
                  row(TOP_K), row(d), row(d),
                  _const_spec(s_gate.shape), _const_spec(s_up.shape), _const_spec(s_down.shape),
                  _const_spec((1, d)), _const_spec((1, d)),
                  pl.BlockSpec(memory_space=pl.ANY)],
        out_specs=[row(d), row(d)],
        out_shape=[jax.ShapeDtypeStruct((m, d), F32), jax.ShapeDtypeStruct((m, d), BF16)],
        scratch_shapes=[pltpu.VMEM((TOP_K, tb * ROW_SUB, ROW_LANES), jnp.uint32), pltpu.SemaphoreType.DMA(())],
        compiler_params=_cparams("arbitrary"),
        name="moe_combine",
    )(pos8, gate_t, h32, h16, s_gate, s_up, s_down, ln_g.reshape(1, d), ln_b.reshape(1, d), ys)


def moe_layer(h32, h16, hpk, router, bias, w_gate, w_up, w_down, layer, s_gate, s_up, s_down, ln_g, ln_b):
    m = h32.shape[0]
    eid8, rank8, gate8, counts = moe_route(h32, router, bias)
    cnt = counts[:, 0].astype(jnp.int32)
    seg = ((cnt + FFN_TM - 1) // FFN_TM) * FFN_TM
    ends = jnp.cumsum(seg)
    offs = ends - seg
    n_rows = m * TOP_K + N_EXPERTS * FFN_TM
    n_tiles = n_rows // FFN_TM
    experts = jnp.arange(N_EXPERTS, dtype=jnp.int32)
    pos8 = jnp.sum(jnp.where(eid8[:, :, None] == experts, offs, 0), axis=-1) + rank8
    tile_start = jnp.arange(n_tiles, dtype=jnp.int32) * FFN_TM
    tile_expert = jnp.minimum(jnp.sum((ends[None, :] <= tile_start[:, None]).astype(jnp.int32), axis=1),
                              N_EXPERTS - 1)
    n_used = (ends[-1:] // FFN_TM).astype(jnp.int32)
    xs = moe_dispatch(hpk, pos8, seg, ends, n_rows)
    ys = moe_experts(xs, tile_expert, n_used, w_gate, w_up, w_down, layer)
    return moe_combine(ys, pos8, gate8.T, h32, h16, s_gate.astype(BF16), s_up.astype(BF16),
                       s_down.astype(BF16), ln_g, ln_b)


def _even_mixer(h16, lb, in_proj, out_proj, hg_gn_w, rw_mix, rw_w0, rw_w2, rw_a0, rw_a2, rw_g2, rw_kk, rw_ka,
                rw_rk, rw_ln_w, rw_ln_b, batch, seq):
    hg_cols = 4 * HG_W
    main_end = hg_cols + 3 * RW_W
    c1 = main_end + RW_DECAY_RANK
    c2 = c1 + RW_A_RANK
    padc = lambda w, n: jnp.pad(w, ((0, 0), (0, n - w.shape[1])))
    w_low = jnp.concatenate([padc(in_proj[:, main_end:c1], 128), padc(in_proj[:, c1:c2], 128),
                             padc(in_proj[:, c2:], 256)], axis=1)
    p_hg = matmul(h16, in_proj[:, :hg_cols].astype(BF16), F32)
    p_main = matmul(h16, in_proj[:, hg_cols:main_end].astype(BF16), F32)
    p_low = matmul(h16, w_low.astype(BF16), F32)
    o_a = hgrn2(p_hg, lb, hg_gn_w, batch, seq)
    o_b, gate_b = rwkv7(p_main, p_low, rw_mix, rw_w0, rw_w2, rw_a0, rw_a2, rw_g2, rw_kk, rw_ka, rw_rk, rw_ln_w,
                        rw_ln_b, batch, seq)
    return o_a, o_b, gate_b


def _odd_mixer(h16, in_proj, conv_w, conv_b, dt_bias, a_log, d_skip, norm_w, batch, seq):
    c1 = SSD_INNER
    c2 = c1 + SSD_CONV_DIM
    z = matmul(h16, in_proj[:, :c1].astype(BF16), BF16)
    xbc = matmul(h16, in_proj[:, c1:c2].astype(BF16), F32)
    dt = matmul(h16, jnp.pad(in_proj[:, c2:], ((0, 0), (0, 128 - SSD_HEADS))).astype(BF16), F32)
    return (ssd(z, xbc, dt, conv_w, conv_b, dt_bias, a_log, d_skip, norm_w, batch, seq),)


def kernel(x, hg_lb, ev_in_proj, ev_out_proj, hg_gn_w, rw_mix, rw_w0, rw_w2, rw_a0, rw_a2, rw_g2, rw_kk, rw_ka,
           rw_rk, rw_ln_w, rw_ln_b, od_in_proj, od_conv_w, od_conv_b, od_dt_bias, od_a_log, od_d, od_norm_w,
           od_out_proj, moe_router, moe_bias, moe_w_gate, moe_w_up, moe_w_down, sh_w_gate, sh_w_up, sh_w_down,
           ln1_g, ln1_b, ln2_g, ln2_b):
    batch, seq, d = x.shape
    lbs = jnp.cumsum(jax.nn.softmax(hg_lb.astype(F32), axis=0), axis=0)
    h32 = x.reshape(batch * seq, d)
    h16 = h32.astype(BF16)
    for l in range(DEPTH):
        if l % 2 == 0:
            e = l // 2
            mix = _even_mixer(h16, lbs[l], ev_in_proj[e], ev_out_proj[e], hg_gn_w[e], rw_mix[e], rw_w0[e],
                              rw_w2[e], rw_a0[e], rw_a2[e], rw_g2[e], rw_kk[e], rw_ka[e], rw_rk[e],
                              rw_ln_w[e], rw_ln_b[e], batch, seq)
            w_out = ev_out_proj[e]
        else:
            o = l // 2
            mix = _odd_mixer(h16, od_in_proj[o], od_conv_w[o], od_conv_b[o], od_dt_bias[o], od_a_log[o],
                             od_d[o], od_norm_w[o], batch, seq)
            w_out = od_out_proj[o]
        h32, h16, hpk = matmul_residual_ln(mix, w_out.astype(BF16), h32, ln1_g[l], ln1_b[l])
        h32, h16 = moe_layer(h32, h16, hpk, moe_router[l], moe_bias[l], moe_w_gate, moe_w_up, moe_w_down, l,
                             sh_w_gate[l], sh_w_up[l], sh_w_down[l], ln2_g[l], ln2_b[l])
    return h32.reshape(batch, seq, d)
```

```python
import functools
import math

import jax
import jax.numpy as jnp
import numpy as np
from jax import lax
from jax.experimental import pallas as pl
from jax.experimental.pallas import tpu as pltpu

F32 = jnp.float32
BF16 = jnp.bfloat16

D_MODEL = 2048
DEPTH = 2
CHUNK = 64

HG_HEADS = 8
HG_D = 128
HG_W = HG_HEADS * HG_D
RW_HEADS = 16
RW_N = 64
RW_W = RW_HEADS * RW_N
RW_DECAY_RANK = 64
RW_A_RANK = 64
RW_GATE_RANK = 160
RW_GN_EPS = 64e-5
RW_LOW_PAD = 512

SSD_INNER = 2 * D_MODEL
SSD_P = 64
SSD_HEADS = SSD_INNER // SSD_P
SSD_GROUPS = 8
SSD_HPG = SSD_HEADS // SSD_GROUPS
SSD_N = 128
SSD_CONV = 4
SSD_BC = SSD_GROUPS * SSD_N
SSD_CONV_DIM = SSD_INNER + 2 * SSD_BC
SSD_GW = SSD_HPG * SSD_P

N_EXPERTS = 64
TOP_K = 8
N_GROUPS = 8
TOPK_GROUPS = 4
EXPERT_DIM = 512
ROUTED_SCALE = 2.5

NORM_EPS = 1e-5
DN_ALPHA = (2 * DEPTH) ** 0.25

VMEM_LIMIT = 56 * 1024 * 1024


def _cparams(*sem):
    return pltpu.CompilerParams(dimension_semantics=sem, vmem_limit_bytes=VMEM_LIMIT)


def _const_spec(shape):
    nd = len(shape)
    return pl.BlockSpec(shape, lambda *_: (0,) * nd, pipeline_mode=pl.Buffered(1))


def _split3(x):
    hi = x.astype(BF16)
    r1 = x - hi.astype(F32)
    mid = r1.astype(BF16)
    lo = (r1 - mid.astype(F32)).astype(BF16)
    return hi, mid, lo


def _dot(a, b):
    return jnp.dot(a, b, preferred_element_type=F32)


def _dot_exact_lhs(m_bf16, x):
    hi, mid, lo = _split3(x)
    return _dot(m_bf16, hi) + _dot(m_bf16, mid) + _dot(m_bf16, lo)


def _dot_exact_rhs(x, m_bf16):
    hi, mid, lo = _split3(x)
    return _dot(hi, m_bf16) + _dot(mid, m_bf16) + _dot(lo, m_bf16)


def _sigmoid(x):
    return 1.0 / (1.0 + jnp.exp(-x))


def _silu(x):
    return x * _sigmoid(x)


def _softplus(x):
    return jnp.maximum(x, 0.0) + jnp.log(1.0 + jnp.exp(-jnp.abs(x)))


def _mm_kernel(x_ref, w_ref, o_ref):
    o_ref[...] = _dot(x_ref[...], w_ref[...]).astype(o_ref.dtype)


def matmul(x, w, out_dtype, tm=1024, tn=512):
    m, k = x.shape
    n = w.shape[1]
    tm = min(tm, m)
    tn = min(tn, n)
    assert m % tm == 0 and n % tn == 0
    return pl.pallas_call(
        _mm_kernel,
        grid=(m // tm, n // tn),
        in_specs=[pl.BlockSpec((tm, k), lambda i, j: (i, 0)),
                  pl.BlockSpec((k, tn), lambda i, j: (0, j))],
        out_specs=pl.BlockSpec((tm, tn), lambda i, j: (i, j)),
        out_shape=jax.ShapeDtypeStruct((m, n), out_dtype),
        compiler_params=_cparams("parallel", "arbitrary"),
        name="matmul",
    )(x, w)


def _layer_norm_rows(y, g, b):
    mu = jnp.mean(y, axis=-1, keepdims=True)
    yc = y - mu
    var = jnp.mean(yc * yc, axis=-1, keepdims=True)
    return yc * lax.rsqrt(var + NORM_EPS) * g + b


def _bf16_bits(x):
    u = lax.bitcast_convert_type(x, jnp.uint32)
    u = u + jnp.uint32(0x7FFF) + ((u >> 16) & jnp.uint32(1))
    return u & jnp.uint32(0xFFFF0000)


def _pack_halves(lo, hi):
    return (_bf16_bits(lo) >> 16) | _bf16_bits(hi)


def _unpack_halves(u):
    lo = lax.bitcast_convert_type(u << 16, F32)
    hi = lax.bitcast_convert_type(u & jnp.uint32(0xFFFF0000), F32)
    return lo, hi


HALF = D_MODEL // 2
ROW_SUB = 8
ROW_LANES = HALF // ROW_SUB


def _store_row_tiles(ref, packed):
    n = packed.shape[0]
    for c in range(ROW_SUB):
        ref[pl.ds(c, n, stride=ROW_SUB), :] = packed[:, c * ROW_LANES:(c + 1) * ROW_LANES]


def _load_row_tiles(ref, n):
    return jnp.concatenate([ref[pl.ds(c, n, stride=ROW_SUB), :] for c in range(ROW_SUB)], axis=1)


def _ln_outputs(y, g_ref, b_ref, o32_ref, o16_ref, opk_ref):
    o = _layer_norm_rows(y, g_ref[...], b_ref[...])
    o32_ref[...] = o
    o16_ref[...] = o.astype(BF16)
    _store_row_tiles(opk_ref, _pack_halves(o[:, :HALF], o[:, HALF:]))


def _mm_ln_kernel(x_ref, w_ref, h_ref, g_ref, b_ref, o32_ref, o16_ref, opk_ref):
    y = _dot(x_ref[...], w_ref[...]) + DN_ALPHA * h_ref[...]
    _ln_outputs(y, g_ref, b_ref, o32_ref, o16_ref, opk_ref)


def _mm2_ln_kernel(xa_ref, xb_ref, gate_ref, w_ref, h_ref, g_ref, b_ref, o32_ref, o16_ref, opk_ref):
    ka = xa_ref.shape[1]
    xb = (xb_ref[...] * gate_ref[...]).astype(BF16)
    y = _dot(xa_ref[...], w_ref[:ka, :]) + _dot(xb, w_ref[ka:, :]) + DN_ALPHA * h_ref[...]
    _ln_outputs(y, g_ref, b_ref, o32_ref, o16_ref, opk_ref)


def matmul_residual_ln(xs, w, h, g, b, tm=256):
    m = xs[0].shape[0]
    k, d = w.shape
    tm = min(tm, m)
    assert m % tm == 0
    row = lambda a: pl.BlockSpec((tm, a.shape[1]), lambda i: (i, 0))
    return pl.pallas_call(
        _mm_ln_kernel if len(xs) == 1 else _mm2_ln_kernel,
        grid=(m // tm,),
        in_specs=[row(a) for a in xs] + [_const_spec((k, d)), row(h), _const_spec((1, d)), _const_spec((1, d))],
        out_specs=[pl.BlockSpec((tm, d), lambda i: (i, 0)),
                   pl.BlockSpec((tm, d), lambda i: (i, 0)),
                   pl.BlockSpec((tm * ROW_SUB, ROW_LANES), lambda i: (i, 0))],
        out_shape=[jax.ShapeDtypeStruct((m, d), F32), jax.ShapeDtypeStruct((m, d), BF16),
                   jax.ShapeDtypeStruct((m * ROW_SUB, ROW_LANES), jnp.uint32)],
        compiler_params=_cparams("parallel"),
        name="matmul_residual_ln",
    )(*xs, w, h, g.reshape(1, d), b.reshape(1, d))


HG_LEVELS = (32, 16, 8, 4, 2, 1)


def _hgrn2_constants():
    t = np.arange(CHUNK)[:, None]
    u = np.arange(CHUNK)[None, :]
    blocks = [(u <= t)]
    for b in HG_LEVELS:
        start = (t // b) * b
        end = start + b - 1
        blocks.append((u >= start) & (u <= t))
        blocks.append((u > t) & (u <= end))
    blocks.append(u > t)
    cm = np.concatenate(blocks, axis=0).astype(np.float32)
    s = u
    masks = [((t // (2 * b)) == (s // (2 * b))) & ((t // b) % 2 == 1) & ((s // b) % 2 == 0) for b in HG_LEVELS]
    return cm, np.stack(masks).astype(np.float32)


HG_HPS = 8


def _hgrn2_kernel(q_ref, f_ref, i_ref, g_ref, lb_ref, gnw_ref, cm_ref, mask_ref, o_ref, st_ref, *, n_chunks):
    @pl.when(pl.program_id(2) == 0)
    def _():
        st_ref[...] = jnp.zeros_like(st_ref)

    lb = lb_ref[...]
    cm = cm_ref[...]
    nt = (((1,), (1,)), ((), ()))
    tail = CHUNK + 2 * CHUNK * len(HG_LEVELS)

    def chunk(c, carry):
        rows = pl.ds(pl.multiple_of(c * CHUNK, CHUNK), CHUNK)
        fp_all = f_ref[rows, :]
        lf_all = jnp.log(lb + (1.0 - lb) * _sigmoid(fp_all))
        k_all = (1.0 - lb) * _sigmoid(-fp_all)
        seg_all = _dot_exact_lhs(cm, lf_all)
        for hh in range(HG_HPS):
            cols = slice(hh * HG_D, (hh + 1) * HG_D)
            q = q_ref[rows, cols]
            v = i_ref[rows, cols]
            g = g_ref[rows, cols]
            k = k_all[:, cols]
            seg = seg_all[:, cols]
            bc = seg[0:CHUNK]
            st = st_ref[hh]
            o = lax.dot_general((q * jnp.exp(bc)).astype(BF16), st.astype(BF16), nt, preferred_element_type=F32)
            att = jnp.zeros((CHUNK, CHUNK), F32)
            for li in range(len(HG_LEVELS)):
                base = CHUNK + 2 * CHUNK * li
                qs = q * jnp.exp(seg[base:base + CHUNK])
                ks = k * jnp.exp(seg[base + CHUNK:base + 2 * CHUNK])
                a_l = lax.dot_general(qs.astype(BF16), ks.astype(BF16), nt, preferred_element_type=F32)
                att = att + a_l * mask_ref[li]
            o = o + _dot(att.astype(BF16), v.astype(BF16)) + jnp.sum(q * k, axis=-1, keepdims=True) * v
            kd = k * jnp.exp(seg[tail:tail + CHUNK])
            st_ref[hh] = st * jnp.exp(bc[CHUNK - 1:CHUNK]) + _dot(v.T.astype(BF16), kd.astype(BF16))
            ms = jnp.mean(o * o, axis=-1, keepdims=True)
            o_ref[rows, cols] = (o * lax.rsqrt(ms + NORM_EPS) * gnw_ref[:, cols] * _silu(g)).astype(o_ref.dtype)
        return carry

    lax.fori_loop(0, n_chunks, chunk, 0)


def hgrn2(p_hg, lb, gn_w, batch, seq, tb=256):
    tb = min(tb, seq)
    nt = seq // tb
    hg = HG_HEADS // HG_HPS
    w = HG_HPS * HG_D
    cm, masks = _hgrn2_constants()
    spec = lambda off: pl.BlockSpec((tb, w), lambda b, h, t: (b * nt + t, off * hg + h))
    return pl.pallas_call(
        functools.partial(_hgrn2_kernel, n_chunks=tb // CHUNK),
        grid=(batch, hg, nt),
        in_specs=[spec(0), spec(1), spec(2), spec(3),
                  pl.BlockSpec((1, w), lambda b, h, t: (0, h)),
                  pl.BlockSpec((1, w), lambda b, h, t: (0, h)),
                  _const_spec(cm.shape), _const_spec(masks.shape)],
        out_specs=pl.BlockSpec((tb, w), lambda b, h, t: (b * nt + t, h)),
        out_shape=jax.ShapeDtypeStruct((batch * seq, HG_W), BF16),
        scratch_shapes=[pltpu.VMEM((HG_HPS, HG_D, HG_D), F32)],
        compiler_params=_cparams("parallel", "parallel", "arbitrary"),
        name="hgrn2",
    )(p_hg, p_hg, p_hg, p_hg, lb.reshape(1, HG_W), gn_w.reshape(1, HG_W),
      jnp.asarray(cm, BF16), jnp.asarray(masks))


def _rwkv_prep_kernel(pm_ref, pmprev_ref, plo_ref, ploprev_ref, mixm_ref, mixl_ref, w0_ref, w2_ref, a0_ref,
                      a2_ref, g2_ref, r_o, k_o, v_o, w_o, a_o, g_o, *, tb, seq):
    first = (pl.program_id(0) * tb) % seq == 0
    keep = jnp.where(first, 0.0, 1.0)

    def shift_mix(cur, prev_blk, mix):
        row = lax.broadcasted_iota(jnp.int32, cur.shape, 0)
        shifted = jnp.where(row == 0, prev_blk[7:8, :] * keep, pltpu.roll(cur, 1, 0))
        return cur + (shifted - cur) * mix

    for idx, out in enumerate((r_o, k_o, v_o)):
        cols = slice(idx * RW_W, (idx + 1) * RW_W)
        out[...] = shift_mix(pm_ref[:, cols], pmprev_ref[:, cols], mixm_ref[:, cols])
    low = shift_mix(plo_ref[...], ploprev_ref[...], mixl_ref[...])
    wl = jnp.tanh(low[:, 0:128]).astype(BF16)
    al = low[:, 128:256].astype(BF16)
    gl = _sigmoid(low[:, 256:512]).astype(BF16)
    w_log = -_softplus(-(w0_ref[...] + _dot(wl, w2_ref[...]))) - 0.5
    w_o[...] = jnp.exp(-jnp.exp(w_log))
    a_o[...] = _sigmoid(a0_ref[...] + _dot(al, a2_ref[...]))
    g_o[...] = _dot(gl, g2_ref[...])


def rwkv_prep(p_main, p_low, mix_main, mix_low, w0, w2p, a0, a2p, g2p, seq, tb=256):
    m = p_main.shape[0]
    tb = min(tb, seq)
    prev = lambda i: (jnp.maximum(i * (tb // 8) - 1, 0), 0)
    out = jax.ShapeDtypeStruct((m, RW_W), F32)
    ospec = pl.BlockSpec((tb, RW_W), lambda i: (i, 0))
    return pl.pallas_call(
        functools.partial(_rwkv_prep_kernel, tb=tb, seq=seq),
        grid=(m // tb,),
        in_specs=[pl.BlockSpec((tb, 3 * RW_W), lambda i: (i, 0)),
                  pl.BlockSpec((8, 3 * RW_W), prev),
                  pl.BlockSpec((tb, RW_LOW_PAD), lambda i: (i, 0)),
                  pl.BlockSpec((8, RW_LOW_PAD), prev),
                  _const_spec((1, 3 * RW_W)), _const_spec((1, RW_LOW_PAD)),
                  _const_spec((1, RW_W)), _const_spec((128, RW_W)),
                  _const_spec((1, RW_W)), _const_spec((128, RW_W)),
                  _const_spec((256, RW_W))],
        out_specs=[ospec] * 6,
        out_shape=[out] * 6,
        compiler_params=_cparams("parallel"),
        name="rwkv_prep",
    )(p_main, p_main, p_low, p_low, mix_main, mix_low, w0, w2p, a0, a2p, g2p)


def _rwkv_scan_kernel(r_ref, k_ref, v_ref, w_ref, a_ref, kkp_ref, kap_ref, rkp_ref, lnw_ref, lnb_ref,
                      o_ref, s_ref, vec_ref, *, tt):
    @pl.when(pl.program_id(0) == 0)
    def _():
        s_ref[...] = jnp.zeros_like(s_ref)

    kkp = kkp_ref[...]
    kap = kap_ref[...]
    rkp = rkp_ref[...]
    lnw = lnw_ref[...]
    lnb = lnb_ref[...]

    def step(t, carry):
        kt = k_ref[t]
        at = a_ref[t]
        rt = r_ref[t]
        vt = v_ref[t]
        kk = kt * kkp
        nrm = jnp.sqrt(jnp.sum(kk * kk, axis=0, keepdims=True))
        kk = kk / jnp.maximum(nrm, 1e-12)
        kh = kt * (1.0 + (at - 1.0) * kap)
        vec_ref[0] = -kk
        vec_ref[1] = kk * at
        vec_ref[2] = kh
        sa = jnp.zeros((RW_N, kt.shape[1]), F32)
        for j in range(RW_N):
            sa = sa + s_ref[j] * vec_ref[0, j:j + 1, :]
        y = jnp.zeros_like(sa)
        for j in range(RW_N):
            sj = (s_ref[j] * w_ref[t, j:j + 1, :] + sa * vec_ref[1, j:j + 1, :]
                  + vt * vec_ref[2, j:j + 1, :])
            s_ref[j] = sj
            y = y + sj * r_ref[t, j:j + 1, :]
        mu = jnp.mean(y, axis=0, keepdims=True)
        yc = y - mu
        var = jnp.mean(yc * yc, axis=0, keepdims=True)
        yn = yc * lax.rsqrt(var + RW_GN_EPS) * lnw + lnb
        bonus = jnp.sum(rt * kh * rkp, axis=0, keepdims=True) * vt
        o_ref[t] = yn + bonus
        return carry

    lax.fori_loop(0, tt, step, 0, unroll=2)


def rwkv_scan(r, k, v, w, a, kkp, kap, rkp, lnw, lnb, tt=16):
    t, n, lanes = r.shape
    tt = min(tt, t)
    blk = pl.BlockSpec((tt, n, lanes), lambda i: (i, 0, 0))
    par = _const_spec((n, lanes))
    return pl.pallas_call(
        functools.partial(_rwkv_scan_kernel, tt=tt),
        grid=(t // tt,),
        in_specs=[blk] * 5 + [par] * 5,
        out_specs=blk,
        out_shape=jax.ShapeDtypeStruct((t, n, lanes), F32),
        scratch_shapes=[pltpu.VMEM((n, n, lanes), F32), pltpu.VMEM((3, n, lanes), F32)],
        compiler_params=_cparams("arbitrary"),
        name="rwkv_scan",
    )(r, k, v, w, a, kkp, kap, rkp, lnw, lnb)


def _time_major(x, batch, seq):
    return x.reshape(batch, seq, RW_HEADS, RW_N).transpose(1, 3, 0, 2).reshape(seq, RW_N, batch * RW_HEADS)


def _token_major(x, batch, seq):
    return x.reshape(seq, RW_N, batch, RW_HEADS).transpose(2, 0, 3, 1).reshape(batch * seq, RW_W)


def _head_param(p, batch):
    return jnp.tile(p.reshape(RW_HEADS, RW_N).T, (1, batch))


def _pad_rows(w, rows):
    return jnp.pad(w, ((0, rows - w.shape[0]), (0, 0)))


def rwkv7(p_main, p_low, mix, w0, w2, a0, a2, g2, k_k, k_a, r_k, ln_w, ln_b, batch, seq):
    mix_main = mix[:3 * RW_W].reshape(1, -1)
    lo = mix[3 * RW_W:]
    c1 = RW_DECAY_RANK
    c2 = c1 + RW_A_RANK
    mix_low = jnp.concatenate([jnp.pad(lo[:c1], (0, 128 - RW_DECAY_RANK)),
                               jnp.pad(lo[c1:c2], (0, 128 - RW_A_RANK)),
                               jnp.pad(lo[c2:], (0, 256 - RW_GATE_RANK))]).reshape(1, RW_LOW_PAD)
    r, k, v, w, a, g = rwkv_prep(
        p_main, p_low, mix_main, mix_low, w0.reshape(1, -1), _pad_rows(w2, 128).astype(BF16),
        a0.reshape(1, -1), _pad_rows(a2, 128).astype(BF16), _pad_rows(g2, 256).astype(BF16), seq)
    tm = lambda x: _time_major(x, batch, seq)
    hp = lambda p: _head_param(p.reshape(-1), batch)
    o = rwkv_scan(tm(r), tm(k), tm(v), tm(w), tm(a), hp(k_k), hp(k_a), hp(r_k), hp(ln_w), hp(ln_b))
    return _token_major(o, batch, seq), g


def _ssd_kernel(z_ref, xbc_ref, dt_ref, cw_ref, cb_ref, dtb_ref, alog_ref, dexp_ref, nw_ref, ex_ref, tri_ref,
                o_ref, xpad_ref, xact_ref, st_ref, y_ref):
    c = pl.program_id(1)

    @pl.when(c == 0)
    def _():
        st_ref[...] = jnp.zeros_like(st_ref)
        xpad_ref[0:8, :] = jnp.zeros((8, SSD_CONV_DIM), F32)

    xpad_ref[8:8 + CHUNK, :] = xbc_ref[...]
    ct = 512
    for j in range(SSD_CONV_DIM // ct):
        cols = slice(j * ct, (j + 1) * ct)
        win = xpad_ref[:, cols]
        acc = cw_ref[0:1, cols] * win
        for tap in range(1, SSD_CONV):
            acc = pltpu.roll(acc, 1, 0) + cw_ref[tap:tap + 1, cols] * win
        xact_ref[:, cols] = _silu(acc[8:] + cb_ref[:, cols])
    xpad_ref[0:8, :] = xpad_ref[CHUNK:CHUNK + 8, :]

    dtv = _softplus(dt_ref[...] + dtb_ref[...])
    da = dtv * (-jnp.exp(alog_ref[...]))
    acum = _dot_exact_lhs(tri_ref[...], da)
    acum_t = acum.T
    ex = ex_ref[...]
    dtx = _dot_exact_rhs(dtv, ex)
    e_in = _dot_exact_rhs(jnp.exp(acum), ex)
    e_out = _dot_exact_rhs(jnp.exp(acum[CHUNK - 1:CHUNK, :] - acum), ex)
    e_last = e_in[CHUNK - 1:CHUNK, :]
    xs = xact_ref[:, 0:SSD_INNER]
    xdt = xs * dtx
    ti = lax.broadcasted_iota(jnp.int32, (CHUNK, CHUNK), 0)
    si = lax.broadcasted_iota(jnp.int32, (CHUNK, CHUNK), 1)
    causal = ti >= si
    lane = lax.broadcasted_iota(jnp.int32, (CHUNK, 128), 1)
    nt = (((1,), (1,)), ((), ()))

    for g in range(SSD_GROUPS):
        bg = xact_ref[:, SSD_INNER + g * SSD_N:SSD_INNER + (g + 1) * SSD_N]
        cg = xact_ref[:, SSD_INNER + SSD_BC + g * SSD_N:SSD_INNER + SSD_BC + (g + 1) * SSD_N]
        cgb = cg.astype(BF16)
        cb = lax.dot_general(cgb, bg.astype(BF16), nt, preferred_element_type=F32)
        gc = slice(g * SSD_GW, (g + 1) * SSD_GW)
        sgt = st_ref[g]
        y_g = _dot(cgb, sgt.astype(BF16)) * e_in[:, gc]
        xdt_g = xdt[:, gc]
        for m in range(SSD_HPG // 2):
            xp = xdt_g[:, m * 128:(m + 1) * 128]
            yp = y_g[:, m * 128:(m + 1) * 128]
            for half in range(2):
                h = g * SSD_HPG + 2 * m + half
                seg = acum[:, h:h + 1] - acum_t[h:h + 1, :]
                lmat = jnp.where(causal, jnp.exp(jnp.minimum(seg, 0.0)), 0.0)
                rhs = jnp.where((lane >= 64) == (half == 1), xp, 0.0)
                yp = yp + _dot((cb * lmat).astype(BF16), rhs.astype(BF16))
            y_ref[:, g * SSD_GW + m * 128:g * SSD_GW + (m + 1) * 128] = yp
        st_ref[g] = sgt * e_last[:, gc] + _dot(bg.T.astype(BF16), (xdt_g * e_out[:, gc]).astype(BF16))

    for g in range(SSD_GROUPS):
        gc = slice(g * SSD_GW, (g + 1) * SSD_GW)
        y = y_ref[:, gc] + dexp_ref[:, gc] * xact_ref[:, gc]
        y = y * _silu(z_ref[:, gc].astype(F32))
        ms = jnp.mean(y * y, axis=-1, keepdims=True)
        o_ref[:, gc] = (y * lax.rsqrt(ms + NORM_EPS) * nw_ref[:, gc]).astype(o_ref.dtype)


def ssd(z, xbc, dt, conv_w, conv_b, dt_bias, a_log, d_skip, norm_w, batch, seq):
    nc = seq // CHUNK
    pad = lambda v: jnp.pad(v.reshape(1, -1), ((0, 0), (0, 128 - SSD_HEADS)))
    ex = np.zeros((128, SSD_INNER), np.float32)
    for h in range(SSD_HEADS):
        ex[h, h * SSD_P:(h + 1) * SSD_P] = 1.0
    tri = np.tril(np.ones((CHUNK, CHUNK), np.float32))
    row = lambda w: pl.BlockSpec((CHUNK, w), lambda b, c: (b * nc + c, 0))
    return pl.pallas_call(
        _ssd_kernel,
        grid=(batch, nc),
        in_specs=[row(SSD_INNER), row(SSD_CONV_DIM), row(128),
                  _const_spec((SSD_CONV, SSD_CONV_DIM)), _const_spec((1, SSD_CONV_DIM)),
                  _const_spec((1, 128)), _const_spec((1, 128)),
                  _const_spec((1, SSD_INNER)), _const_spec((1, SSD_INNER)),
                  _const_spec((128, SSD_INNER)), _const_spec((CHUNK, CHUNK))],
        out_specs=row(SSD_INNER),
        out_shape=jax.ShapeDtypeStruct((batch * seq, SSD_INNER), BF16),
        scratch_shapes=[pltpu.VMEM((CHUNK + 8, SSD_CONV_DIM), F32),
                        pltpu.VMEM((CHUNK, SSD_CONV_DIM), F32),
                        pltpu.VMEM((SSD_GROUPS, SSD_N, SSD_GW), F32),
                        pltpu.VMEM((CHUNK, SSD_INNER), F32)],
        compiler_params=_cparams("parallel", "arbitrary"),
        name="ssd",
    )(z, xbc, dt, conv_w, conv_b.reshape(1, -1), pad(dt_bias), pad(a_log),
      jnp.repeat(d_skip, SSD_P).reshape(1, -1), norm_w.reshape(1, -1), jnp.asarray(ex, BF16),
      jnp.asarray(tri, BF16))


FFN_TM = 256


def _router_kernel(x_ref, rt_ref, bias_ref, up_ref, ls_ref, eid_o, rank_o, gate_o, cnt_o, carry_ref, *, tb):
    @pl.when(pl.program_id(0) == 0)
    def _():
        carry_ref[...] = jnp.zeros_like(carry_ref)

    nt = (((1,), (1,)), ((), ()))
    x = x_ref[...]
    rt = rt_ref[...]
    xh = x.astype(BF16)
    xl = (x - xh.astype(F32)).astype(BF16)
    rh = rt.astype(BF16)
    rl = (rt - rh.astype(F32)).astype(BF16)
    dg = lambda a, b: lax.dot_general(a, b, nt, preferred_element_type=F32)
    logits = dg(rh, xh) + dg(rh, xl) + dg(rl, xh)
    scores = _sigmoid(logits)
    reps = tb // 128
    wide = lambda a: jnp.concatenate([a] * reps, axis=1) if reps > 1 else a
    biased = scores + wide(bias_ref[...])
    neg = -jnp.inf

    io8 = lax.broadcasted_iota(jnp.int32, (8, tb), 0)
    blocks, gs = [], []
    for g in range(N_GROUPS):
        blk = biased[8 * g:8 * g + 8, :]
        m1 = jnp.max(blk, axis=0, keepdims=True)
        first = jnp.min(jnp.where(blk == m1, io8, 8), axis=0, keepdims=True)
        m2 = jnp.max(jnp.where(io8 == first, neg, blk), axis=0, keepdims=True)
        blocks.append(blk)
        gs.append(m1 + m2)
    masked = []
    for g in range(N_GROUPS):
        ahead = jnp.zeros((1, tb), jnp.int32)
        for o in range(N_GROUPS):
            if o == g:
                continue
            beats = (gs[o] > gs[g]) | ((gs[o] == gs[g]) & (o < g))
            ahead = ahead + jnp.where(beats, 1, 0)
        masked.append(jnp.where(ahead < TOPK_GROUPS, blocks[g], neg))
    masked = jnp.concatenate(masked, axis=0)
    eidx = lax.broadcasted_iota(jnp.int32, (N_EXPERTS, tb), 0)
    ahead = jnp.zeros((N_EXPERTS, tb), jnp.int32)
    for o in range(N_EXPERTS):
        row = masked[o:o + 1, :]
        beats = (row > masked) | ((row == masked) & (eidx > o))
        ahead = ahead + jnp.where(beats, 1, 0)
    sel = ahead < TOP_K
    self_ = jnp.where(sel, 1.0, 0.0)
    wts = jnp.where(sel, scores, 0.0)
    gate = wts / jnp.sum(wts, axis=0, keepdims=True) * ROUTED_SCALE

    selb = self_.astype(BF16)
    carry = carry_ref[...]
    rank = _dot(selb, up_ref[...]) + wide(carry)
    new_carry = carry + _dot(selb, jnp.ones((tb, 128), BF16))
    carry_ref[...] = new_carry
    cnt_o[...] = new_carry
    before = _dot(ls_ref[...], selb)
    eidf = eidx.astype(F32)
    for j in range(TOP_K):
        hit = sel & (before == float(j))
        pick = lambda a: jnp.sum(jnp.where(hit, a, 0.0), axis=0, keepdims=True)
        eid_o[j:j + 1, :] = pick(eidf).astype(jnp.int32)
        rank_o[j:j + 1, :] = pick(rank).astype(jnp.int32)
        gate_o[j:j + 1, :] = pick(gate)


def moe_route(h32, router, bias, tb=256):
    m = h32.shape[0]
    tb = min(tb, m)
    up = np.triu(np.ones((tb, tb), np.float32), k=1)
    ls = np.tril(np.ones((N_EXPERTS, N_EXPERTS), np.float32), k=-1)
    slot = lambda dt: jax.ShapeDtypeStruct((TOP_K, m), dt)
    return pl.pallas_call(
        functools.partial(_router_kernel, tb=tb),
        grid=(m // tb,),
        in_specs=[pl.BlockSpec((tb, D_MODEL), lambda i: (i, 0)),
                  _const_spec((N_EXPERTS, D_MODEL)), _const_spec((N_EXPERTS, 128)),
                  _const_spec((tb, tb)), _const_spec((N_EXPERTS, N_EXPERTS))],
        out_specs=[pl.BlockSpec((TOP_K, tb), lambda i: (0, i))] * 3
                  + [pl.BlockSpec((N_EXPERTS, 128), lambda i: (0, 0))],
        out_shape=[slot(jnp.int32), slot(jnp.int32), slot(F32),
                   jax.ShapeDtypeStruct((N_EXPERTS, 128), F32)],
        scratch_shapes=[pltpu.VMEM((N_EXPERTS, 128), F32)],
        compiler_params=_cparams("arbitrary"),
        name="moe_route",
    )(h32, router.T, jnp.broadcast_to(bias.reshape(N_EXPERTS, 1), (N_EXPERTS, 128)),
      jnp.asarray(up, BF16), jnp.asarray(ls, BF16))


def _frame_rows(p):
    return pl.ds(pl.multiple_of(p * ROW_SUB, ROW_SUB), ROW_SUB)


def _dispatch_kernel(seg_ref, ends_ref, pos_ref, x_ref, xs_out, zero_ref, sem, zsem, *, tb):
    tile_rows = FFN_TM * ROW_SUB

    @pl.when(pl.program_id(0) == 0)
    def _():
        zero_ref[...] = jnp.zeros_like(zero_ref)

        def fill(e):
            start = pl.multiple_of((ends_ref[e] - FFN_TM) * ROW_SUB, tile_rows)
            return pltpu.make_async_copy(zero_ref, xs_out.at[pl.ds(start, tile_rows), :], zsem)

        def start_fill(e, c):
            @pl.when(seg_ref[e] > 0)
            def _():
                fill(e).start()
            return c

        def wait_fill(e, c):
            @pl.when(seg_ref[e] > 0)
            def _():
                fill(e).wait()
            return c

        lax.fori_loop(0, N_EXPERTS, start_fill, 0)
        lax.fori_loop(0, N_EXPERTS, wait_fill, 0)

    def copy(t, p):
        return pltpu.make_async_copy(x_ref.at[_frame_rows(t), :], xs_out.at[_frame_rows(p), :], sem)

    def issue(t, c):
        for j in range(TOP_K):
            copy(t, pos_ref[j, t]).start(priority=j % 2)
        return c

    lax.fori_loop(0, tb, issue, 0)

    def drain(t, c):
        for j in range(TOP_K):
            copy(0, 0).wait()
        return c

    lax.fori_loop(0, tb, drain, 0)


def moe_dispatch(xpk, pos8, seg, ends, n_rows, tb=256):
    m = xpk.shape[0] // ROW_SUB
    tb = min(tb, m)
    grid_spec = pltpu.PrefetchScalarGridSpec(
        num_scalar_prefetch=2,
        grid=(m // tb,),
        in_specs=[pl.BlockSpec((TOP_K, tb), lambda i, sg, en: (0, i), memory_space=pltpu.SMEM),
                  pl.BlockSpec((tb * ROW_SUB, ROW_LANES), lambda i, sg, en: (i, 0))],
        out_specs=pl.BlockSpec(memory_space=pl.ANY),
        scratch_shapes=[pltpu.VMEM((FFN_TM * ROW_SUB, ROW_LANES), xpk.dtype),
                        pltpu.SemaphoreType.DMA(()), pltpu.SemaphoreType.DMA(())],
    )
    return pl.pallas_call(
        functools.partial(_dispatch_kernel, tb=tb),
        grid_spec=grid_spec,
        out_shape=jax.ShapeDtypeStruct((n_rows * ROW_SUB, ROW_LANES), xpk.dtype),
        compiler_params=_cparams("arbitrary"),
        name="moe_dispatch",
    )(seg, ends, pos8, xpk)


def _ffn_kernel(te_ref, nu_ref, xs_ref, wg_ref, wu_ref, wd_ref, ys_ref, wg16_ref, wu16_ref, wd16_ref):
    i = pl.program_id(0)

    @pl.when((i == 0) | (te_ref[i] != te_ref[jnp.maximum(i - 1, 0)]))
    def _():
        wg16_ref[...] = wg_ref[0].astype(BF16)
        wu16_ref[...] = wu_ref[0].astype(BF16)
        wd16_ref[...] = wd_ref[0].astype(BF16)

    @pl.when(i < nu_ref[0])
    def _():
        lo, hi = _unpack_halves(_load_row_tiles(xs_ref, FFN_TM))
        lo = lo.astype(BF16)
        hi = hi.astype(BF16)
        hg = _dot(lo, wg16_ref[:HALF, :]) + _dot(hi, wg16_ref[HALF:, :])
        hu = _dot(lo, wu16_ref[:HALF, :]) + _dot(hi, wu16_ref[HALF:, :])
        y = _dot((_silu(hg) * hu).astype(BF16), wd16_ref[...])
        _store_row_tiles(ys_ref, _pack_halves(y[:, :HALF], y[:, HALF:]))


def moe_experts(xs, tile_expert, n_used, w_gate, w_up, w_down, layer):
    n_tiles = xs.shape[0] // (FFN_TM * ROW_SUB)
    wspec = lambda shp: pl.BlockSpec((None, 1) + shp, lambda i, te, nu: (layer, te[i], 0, 0))
    rows = pl.BlockSpec((FFN_TM * ROW_SUB, ROW_LANES), lambda i, te, nu: (jnp.minimum(i, nu[0] - 1), 0))
    grid_spec = pltpu.PrefetchScalarGridSpec(
        num_scalar_prefetch=2,
        grid=(n_tiles,),
        in_specs=[rows, wspec((D_MODEL, EXPERT_DIM)), wspec((D_MODEL, EXPERT_DIM)),
                  wspec((EXPERT_DIM, D_MODEL))],
        out_specs=rows,
        scratch_shapes=[pltpu.VMEM((D_MODEL, EXPERT_DIM), BF16), pltpu.VMEM((D_MODEL, EXPERT_DIM), BF16),
                        pltpu.VMEM((EXPERT_DIM, D_MODEL), BF16)],
    )
    return pl.pallas_call(
        _ffn_kernel,
        grid_spec=grid_spec,
        out_shape=jax.ShapeDtypeStruct(xs.shape, xs.dtype),
        compiler_params=_cparams("arbitrary"),
        name="moe_experts",
    )(tile_expert, n_used, xs, w_gate, w_up, w_down)


def _combine_kernel(pos_ref, posn_ref, gate_ref, h32_ref, h16_ref, sg_ref, su_ref, sd_ref, lng_ref, lnb_ref,
                    ys_hbm, o32_ref, o16_ref, buf_ref, sems, *, tb):
    i = pl.program_id(0)
    slot = lax.rem(i, 2)
    sem = sems.at[slot]

    def copy(p, s, j, t, sm):
        return pltpu.make_async_copy(ys_hbm.at[_frame_rows(p), :], buf_ref.at[s, j, _frame_rows(t), :], sm)

    def issue_block(pref, s):
        def issue(t, c):
            for j in range(TOP_K):
                copy(pref[j, t], s, j, t, sems.at[s]).start(priority=j % 2)
            return c
        lax.fori_loop(0, tb, issue, 0)

    @pl.when(i == 0)
    def _():
        issue_block(pos_ref, slot)

    @pl.when(i + 1 < pl.num_programs(0))
    def _():
        issue_block(posn_ref, 1 - slot)

    x = h16_ref[...]
    hdn = (_silu(_dot(x, sg_ref[...])) * _dot(x, su_ref[...])).astype(BF16)
    shared = _dot(hdn, sd_ref[...])
    acc_lo = shared[:, :HALF] + DN_ALPHA * h32_ref[:, :HALF]
    acc_hi = shared[:, HALF:] + DN_ALPHA * h32_ref[:, HALF:]

    def drain(t, c):
        for j in range(TOP_K):
            copy(0, slot, 0, 0, sem).wait()
        return c

    lax.fori_loop(0, tb, drain, 0)

    for j in range(TOP_K):
        lo, hi = _unpack_halves(_load_row_tiles(buf_ref.at[slot, j], tb))
        gj = gate_ref[:, j:j + 1]
        acc_lo = acc_lo + gj * lo
        acc_hi = acc_hi + gj * hi
    mu = (jnp.sum(acc_lo, axis=-1, keepdims=True) + jnp.sum(acc_hi, axis=-1, keepdims=True)) / D_MODEL
    c_lo = acc_lo - mu
    c_hi = acc_hi - mu
    var = (jnp.sum(c_lo * c_lo, axis=-1, keepdims=True) + jnp.sum(c_hi * c_hi, axis=-1, keepdims=True)) / D_MODEL
    inv = lax.rsqrt(var + NORM_EPS)
    o_lo = c_lo * inv * lng_ref[:, :HALF] + lnb_ref[:, :HALF]
    o_hi = c_hi * inv * lng_ref[:, HALF:] + lnb_ref[:, HALF:]
    o32_ref[:, :HALF] = o_lo
    o32_ref[:, HALF:] = o_hi
    o16_ref[:, :HALF] = o_lo.astype(BF16)
    o16_ref[:, HALF:] = o_hi.astype(BF16)


def moe_combine(ys, pos8, gate_t, h32, h16, s_gate, s_up, s_down, ln_g, ln_b, tb=128):
    m, d = h32.shape
    tb = min(tb, m)
    row = lambda w: pl.BlockSpec((tb, w), lambda i: (i, 0))
    last = m // tb - 1
    return pl.pallas_call(
        functools.partial(_combine_kernel, tb=tb),
        grid=(m // tb,),
        in_specs=[pl.BlockSpec((TOP_K, tb), lambda i: (0, i), memory_space=pltpu.SMEM),
                  pl.BlockSpec((TOP_K, tb), lambda i: (0, jnp.minimum(i + 1, last)), memory_space=pltpu.SMEM),
                  row(TOP_K), row(d), row(d),
                  _const_spec(s_gate.shape), _const_spec(s_up.shape), _const_spec(s_down.shape),
                  _const_spec((1, d)), _const_spec((1, d)),
                  pl.BlockSpec(memory_space=pl.ANY)],
        out_specs=[row(d), row(d)],
        out_shape=[jax.ShapeDtypeStruct((m, d), F32), jax.ShapeDtypeStruct((m, d), BF16)],
        scratch_shapes=[pltpu.VMEM((2, TOP_K, tb * ROW_SUB, ROW_LANES), jnp.uint32),
                        pltpu.SemaphoreType.DMA((2,))],
        compiler_params=_cparams("arbitrary"),
        name="moe_combine",
    )(pos8, pos8, gate_t, h32, h16, s_gate, s_up, s_down, ln_g.reshape(1, d), ln_b.reshape(1, d), ys)


def moe_layer(h32, h16, hpk, router, bias, w_gate, w_up, w_down, layer, s_gate, s_up, s_down, ln_g, ln_b):
    m = h32.shape[0]
    eid8, rank8, gate8, counts = moe_route(h32, router, bias)
    cnt = counts[:, 0].astype(jnp.int32)
    seg = ((cnt + FFN_TM - 1) // FFN_TM) * FFN_TM
    ends = jnp.cumsum(seg)
    offs = ends - seg
    n_rows = m * TOP_K + N_EXPERTS * FFN_TM
    n_tiles = n_rows // FFN_TM
    experts = jnp.arange(N_EXPERTS, dtype=jnp.int32)
    pos8 = jnp.sum(jnp.where(eid8[:, :, None] == experts, offs, 0), axis=-1) + rank8
    tile_start = jnp.arange(n_tiles, dtype=jnp.int32) * FFN_TM
    tile_expert = jnp.minimum(jnp.sum((ends[None, :] <= tile_start[:, None]).astype(jnp.int32), axis=1),
                              N_EXPERTS - 1)
    n_used = (ends[-1:] // FFN_TM).astype(jnp.int32)
    xs = moe_dispatch(hpk, pos8, seg, ends, n_rows)
    ys = moe_experts(xs, tile_expert, n_used, w_gate, w_up, w_down, layer)
    return moe_combine(ys, pos8, gate8.T, h32, h16, s_gate.astype(BF16), s_up.astype(BF16),
                       s_down.astype(BF16), ln_g, ln_b)


def _even_mixer(h16, lb, in_proj, out_proj, hg_gn_w, rw_mix, rw_w0, rw_w2, rw_a0, rw_a2, rw_g2, rw_kk, rw_ka,
                rw_rk, rw_ln_w, rw_ln_b, batch, seq):
    hg_cols = 4 * HG_W
    main_end = hg_cols + 3 * RW_W
    c1 = main_end + RW_DECAY_RANK
    c2 = c1 + RW_A_RANK
    padc = lambda w, n: jnp.pad(w, ((0, 0), (0, n - w.shape[1])))
    w_low = jnp.concatenate([padc(in_proj[:, main_end:c1], 128), padc(in_proj[:, c1:c2], 128),
                             padc(in_proj[:, c2:], 256)], axis=1)
    p_hg = matmul(h16, in_proj[:, :hg_cols].astype(BF16), F32)
    p_main = matmul(h16, in_proj[:, hg_cols:main_end].astype(BF16), F32)
    p_low = matmul(h16, w_low.astype(BF16), F32)
    o_a = hgrn2(p_hg, lb, hg_gn_w, batch, seq)
    o_b, gate_b = rwkv7(p_main, p_low, rw_mix, rw_w0, rw_w2, rw_a0, rw_a2, rw_g2, rw_kk, rw_ka, rw_rk, rw_ln_w,
                        rw_ln_b, batch, seq)
    return o_a, o_b, gate_b


def _odd_mixer(h16, in_proj, conv_w, conv_b, dt_bias, a_log, d_skip, norm_w, batch, seq):
    c1 = SSD_INNER
    c2 = c1 + SSD_CONV_DIM
    z = matmul(h16, in_proj[:, :c1].astype(BF16), BF16)
    xbc = matmul(h16, in_proj[:, c1:c2].astype(BF16), F32)
    dt = matmul(h16, jnp.pad(in_proj[:, c2:], ((0, 0), (0, 128 - SSD_HEADS))).astype(BF16), F32)
    return (ssd(z, xbc, dt, conv_w, conv_b, dt_bias, a_log, d_skip, norm_w, batch, seq),)


def kernel(x, hg_lb, ev_in_proj, ev_out_proj, hg_gn_w, rw_mix, rw_w0, rw_w2, rw_a0, rw_a2, rw_g2, rw_kk, rw_ka,
           rw_rk, rw_ln_w, rw_ln_b, od_in_proj, od_conv_w, od_conv_b, od_dt_bias, od_a_log, od_d, od_norm_w,
           od_out_proj, moe_router, moe_bias, moe_w_gate, moe_w_up, moe_w_down, sh_w_gate, sh_w_up, sh_w_down,
           ln1_g, ln1_b, ln2_g, ln2_b):
    batch, seq, d = x.shape
    lbs = jnp.cumsum(jax.nn.softmax(hg_lb.astype(F32), axis=0), axis=0)
    h32 = x.reshape(batch * seq, d)
    h16 = h32.astype(BF16)
    for l in range(DEPTH):
        if l % 2 == 0:
            e = l // 2
            mix = _even_mixer(h16, lbs[l], ev_in_proj[e], ev_out_proj[e], hg_gn_w[e], rw_mix[e], rw_w0[e],
                              rw_w2[e], rw_a0[e], rw_a2[e], rw_g2[e], rw_kk[e], rw_ka[e], rw_rk[e],
                              rw_ln_w[e], rw_ln_b[e], batch, seq)
            w_out = ev_out_proj[e]
        else:
            o = l // 2
            mix = _odd_mixer(h16, od_in_proj[o], od_conv_w[o], od_conv_b[o], od_dt_bias[o], od_a_log[o],
                             od_d[o], od_norm_w[o], batch, seq)
            w_out = od_out_proj[o]
        h32, h16, hpk = matmul_residual_ln(mix, w_out.astype(BF16), h32, ln1_g[l], ln1_b[l])
        h32, h16 = moe_layer(h32, h16, hpk, moe_router[l], moe_bias[l], moe_w_gate, moe_w_up, moe_w_down, l,
                             sh_w_gate[l], sh_w_up[l], sh_w_down[l], ln2_g[l], ln2_b[l])
    return h32.reshape(batch, seq, d)
```
